```python
import jax, jax.numpy as jnp
from jax import lax
import numpy as np

D_MODEL = 2048
BATCH = 4
SEQ = 2048
DEPTH = 4
DEC_BATCH = 8
DEC_SEQ = 8
PAST_LEN = 16384
PAGE_SIZE = 128

N_MIXERS = 2
N_A_LAYERS = (DEPTH + 1) // 2
N_B_LAYERS = DEPTH // 2
WINDOWS = (128, 512, 2048)
DILATIONS = (1, 4, 16)
N_GROUPS = 3
H_G = 4
HD_A = 128
BLOCK = 128
CONV_W = 3
N_MEM = 256
H_M = 4
HD_M = D_MODEL // H_M
D_FF = 5632
EPS = 1e-6
NEG = -1e30

kernel_name = "hybrid_dilated_shortconv_decoder_step"


def _rmsnorm(x, g):
    x32 = x.astype(jnp.float32)
    y = x32 * lax.rsqrt(jnp.mean(x32 * x32, axis=-1, keepdims=True) + EPS)
    return (y * g.astype(jnp.float32)).astype(x.dtype)


def _causal_dwconv(u, prev, w):
    ext = jnp.concatenate([prev.astype(u.dtype), u], axis=1)
    T = u.shape[1]
    out = w[0] * ext[:, 0:T]
    for k in range(1, CONV_W):
        out = out + w[k] * ext[:, k:k + T]
    return out, ext[:, ext.shape[1] - (CONV_W - 1):]


def _dilated_band(q, k, v, dil, n_back):
    N, S, H, E = q.shape
    L = S // dil
    nb = -(-L // BLOCK)
    Lp = nb * BLOCK

    def prep(a):
        a = a.astype(jnp.float32).reshape(N, L, dil, H, E).transpose(0, 2, 1, 3, 4)
        a = jnp.pad(a, ((0, 0), (0, 0), (0, Lp - L), (0, 0), (0, 0)))
        return a.reshape(N, dil, nb, BLOCK, H, E)

    def band(a):
        prev = jnp.pad(a, ((0, 0), (0, 0), (1, 0), (0, 0), (0, 0), (0, 0)))[:, :, :nb]
        return jnp.concatenate([prev, a], axis=3)

    qb = prep(q)
    kk, vv = band(prep(k)), band(prep(v))
    s = jnp.einsum('nrbqhe,nrbkhe->nrbhqk', qb, kk) * (HD_A ** -0.5)
    qi = jnp.arange(BLOCK)[:, None] + BLOCK
    ki = jnp.arange(2 * BLOCK)[None, :]
    dist = qi - ki
    kpos = jnp.arange(nb)[:, None, None] * BLOCK - BLOCK + ki[None]
    valid = ((dist >= 0) & (dist <= n_back))[None] & (kpos >= 0)
    s = jnp.where(valid[None, None, :, None], s, NEG)
    lse = jax.nn.logsumexp(s, axis=-1)
    p = jnp.exp(s - lse[..., None])
    o = jnp.einsum('nrbhqk,nrbkhe->nrbqhe', p, vv)
    o = o.reshape(N, dil, Lp, H, E)[:, :, :L].transpose(0, 2, 1, 3, 4).reshape(N, S, H, E)
    lse = lse.transpose(0, 1, 2, 4, 3).reshape(N, dil, Lp, H)[:, :, :L]
    lse = lse.transpose(0, 2, 1, 3).reshape(N, S, H)
    return o, lse


def _dilated_gather(q, k_ext, v_ext, n_buf, dil, n_back):
    T = q.shape[1]
    idx = n_buf + jnp.arange(T)[:, None] - dil * jnp.arange(n_back + 1)[None, :]
    valid = idx >= 0
    idx = jnp.maximum(idx, 0)
    kg = k_ext.astype(jnp.float32)[:, idx]
    vg = v_ext.astype(jnp.float32)[:, idx]
    s = jnp.einsum('nthe,ntmhe->nthm', q.astype(jnp.float32), kg) * (HD_A ** -0.5)
    s = jnp.where(valid[None, :, None, :], s, NEG)
    lse = jax.nn.logsumexp(s, axis=-1)
    p = jnp.exp(s - lse[..., None])
    return jnp.einsum('nthm,ntmhe->nthe', p, vg), lse


def _merge_groups(outs, lses, w_out, dtype):
    wts = jax.nn.softmax(jnp.stack(lses), axis=0)
    o = jnp.einsum('gnth,gnthe->nthe', wts, jnp.stack(outs))
    N, T = o.shape[0], o.shape[1]
    return o.reshape(N, T, H_G * HD_A).astype(dtype) @ w_out


def _mixer_a_prompt(h, w_in, w_out):
    N, S, _ = h.shape
    qkv = (h @ w_in).reshape(N, S, N_GROUPS, 3, H_G, HD_A)
    outs, lses, bufs = [], [], []
    for g in range(N_GROUPS):
        o, lse = _dilated_band(qkv[:, :, g, 0], qkv[:, :, g, 1], qkv[:, :, g, 2],
                               DILATIONS[g], WINDOWS[g] // DILATIONS[g])
        outs.append(o)
        lses.append(lse)
        bufs.append(qkv[:, S - min(WINDOWS[g], S):, g, 1:])
    return _merge_groups(outs, lses, w_out, h.dtype), bufs


def _mixer_a_sample(h, bufs, w_in, w_out):
    N, T, _ = h.shape
    qkv = (h @ w_in).reshape(N, T, N_GROUPS, 3, H_G, HD_A)
    outs, lses, new_bufs = [], [], []
    for g in range(N_GROUPS):
        buf = bufs[g].astype(h.dtype)
        n_buf = buf.shape[1]
        ext = jnp.concatenate([buf, qkv[:, :, g, 1:]], axis=1)
        o, lse = _dilated_gather(qkv[:, :, g, 0], ext[:, :, 0], ext[:, :, 1], n_buf,
                                 DILATIONS[g], WINDOWS[g] // DILATIONS[g])
        outs.append(o)
        lses.append(lse)
        keep = min(WINDOWS[g], PAST_LEN + T)
        new_bufs.append(ext[:, ext.shape[1] - keep:])
    return _merge_groups(outs, lses, w_out, h.dtype), new_bufs


def _mixer_b(h, prev, w_in, w_conv, w_out):
    N, T, _ = h.shape
    u = (h @ w_in).reshape(N, T, 3, D_MODEL)
    gate_b, gate_c, val = u[:, :, 0], u[:, :, 1], u[:, :, 2]
    c, st = _causal_dwconv(gate_c * val, prev, w_conv)
    return (gate_b * c) @ w_out, st


def _mem_kv(mem, g, w_kv):
    N, M, _ = mem.shape
    return (_rmsnorm(mem, g) @ w_kv).reshape(N, M, 2, H_M, HD_M)


def _mem_attend(h, kv, w_q, w_o):
    N, T, _ = h.shape
    q = (h @ w_q).reshape(N, T, H_M, HD_M).astype(jnp.float32)
    k = kv[:, :, 0].astype(jnp.float32)
    v = kv[:, :, 1].astype(jnp.float32)
    s = jnp.einsum('nthe,nmhe->nhtm', q, k) * (HD_M ** -0.5)
    p = jax.nn.softmax(s, axis=-1)
    o = jnp.einsum('nhtm,nmhe->nthe', p, v)
    return o.reshape(N, T, H_M * HD_M).astype(h.dtype) @ w_o


def _conv_ffn(h, prev, w_up, w_conv, w_down):
    u, st = _causal_dwconv(h @ w_up, prev, w_conv)
    a, b = jnp.split(u, 2, axis=-1)
    return (jax.nn.silu(a) * b) @ w_down, st


def setup_inputs(seed: int = 0) -> dict:
    key = jax.random.key(seed)
    ks = jax.random.split(key, 32)
    f32 = jnp.float32

    def nrm(k, shape, scale=1.0):
        return jax.random.normal(k, shape, f32) * scale

    def gain(k, shape):
        return 1.0 + 0.05 * jax.random.normal(k, shape, f32)

    nbuf = [min(w, PAST_LEN) for w in WINDOWS]
    qkv_a = N_GROUPS * 3 * H_G * HD_A
    return {
        "x_prompt": nrm(ks[0], (BATCH, SEQ, D_MODEL)),
        "x_sample": nrm(ks[1], (DEC_BATCH, DEC_SEQ, D_MODEL)),
        "mem_prompt": nrm(ks[2], (BATCH, N_MEM, D_MODEL)),
        "cache_win0_kv": nrm(ks[3], (N_A_LAYERS, DEC_BATCH, nbuf[0], 2, H_G, HD_A)),
        "cache_win1_kv": nrm(ks[4], (N_A_LAYERS, DEC_BATCH, nbuf[1], 2, H_G, HD_A)),
        "cache_win2_kv": nrm(ks[5], (N_A_LAYERS, DEC_BATCH, nbuf[2], 2, H_G, HD_A)),
        "state_conv_b": nrm(ks[6], (N_B_LAYERS, DEC_BATCH, CONV_W - 1, D_MODEL)),
        "state_ffn_conv": nrm(ks[7], (DEPTH, DEC_BATCH, CONV_W - 1, 2 * D_FF)),
        "cache_mem_kv": nrm(ks[8], (DEPTH, DEC_BATCH, N_MEM, 2, H_M, HD_M)),
        "g_mix": gain(ks[9], (DEPTH, D_MODEL)),
        "g_mem_q": gain(ks[10], (DEPTH, D_MODEL)),
        "g_mem_kv": gain(ks[11], (DEPTH, D_MODEL)),
        "g_ffn": gain(ks[12], (DEPTH, D_MODEL)),
        "g_final": gain(ks[13], (D_MODEL,)),
        "w_in_a": nrm(ks[14], (N_A_LAYERS, D_MODEL, qkv_a), D_MODEL ** -0.5),
        "w_out_a": nrm(ks[15], (N_A_LAYERS, H_G * HD_A, D_MODEL), (H_G * HD_A) ** -0.5),
        "w_in_b": nrm(ks[16], (N_B_LAYERS, D_MODEL, 3 * D_MODEL), D_MODEL ** -0.5),
        "conv_b": nrm(ks[17], (N_B_LAYERS, CONV_W, D_MODEL), CONV_W ** -0.5),
        "w_out_b": nrm(ks[18], (N_B_LAYERS, D_MODEL, D_MODEL), D_MODEL ** -0.5),
        "w_q_mem": nrm(ks[19], (DEPTH, D_MODEL, H_M * HD_M), D_MODEL ** -0.5),
        "w_kv_mem": nrm(ks[20], (DEPTH, D_MODEL, 2 * H_M * HD_M), D_MODEL ** -0.5),
        "w_o_mem": nrm(ks[21], (DEPTH, H_M * HD_M, D_MODEL), (H_M * HD_M) ** -0.5),
        "w_up": nrm(ks[22], (DEPTH, D_MODEL, 2 * D_FF), D_MODEL ** -0.5),
        "conv_ffn": nrm(ks[23], (DEPTH, CONV_W, 2 * D_FF), CONV_W ** -0.5),
        "w_down": nrm(ks[24], (DEPTH, D_FF, D_MODEL), D_FF ** -0.5),
    }


def reference(x_prompt, x_sample, mem_prompt, cache_win0_kv, cache_win1_kv, cache_win2_kv,
              state_conv_b, state_ffn_conv, cache_mem_kv, g_mix, g_mem_q, g_mem_kv, g_ffn, g_final,
              w_in_a, w_out_a, w_in_b, conv_b, w_out_b, w_q_mem, w_kv_mem, w_o_mem,
              w_up, conv_ffn, w_down):
    n_p = x_prompt.shape[0]
    dt = x_prompt.dtype
    yp, ys = x_prompt, x_sample
    win_p = [[] for _ in range(N_GROUPS)]
    win_s = [[] for _ in range(N_GROUPS)]
    conv_p, conv_s, ffn_p, ffn_s, mem_p = [], [], [], [], []
    for i in range(DEPTH):
        j = i // N_MIXERS
        hp, hs = _rmsnorm(yp, g_mix[i]), _rmsnorm(ys, g_mix[i])
        if i % N_MIXERS == 0:
            op, bp = _mixer_a_prompt(hp, w_in_a[j], w_out_a[j])
            osm, bs = _mixer_a_sample(hs, (cache_win0_kv[j], cache_win1_kv[j], cache_win2_kv[j]),
                                      w_in_a[j], w_out_a[j])
            for g in range(N_GROUPS):
                win_p[g].append(bp[g])
                win_s[g].append(bs[g])
        else:
            zp = jnp.zeros((n_p, CONV_W - 1, D_MODEL), dt)
            op, sp = _mixer_b(hp, zp, w_in_b[j], conv_b[j], w_out_b[j])
            osm, ss = _mixer_b(hs, state_conv_b[j], w_in_b[j], conv_b[j], w_out_b[j])
            conv_p.append(sp)
            conv_s.append(ss)
        yp = yp + op
        ys = ys + osm
        kv_p = _mem_kv(mem_prompt, g_mem_kv[i], w_kv_mem[i])
        mem_p.append(kv_p)
        yp = yp + _mem_attend(_rmsnorm(yp, g_mem_q[i]), kv_p, w_q_mem[i], w_o_mem[i])
        ys = ys + _mem_attend(_rmsnorm(ys, g_mem_q[i]), cache_mem_kv[i], w_q_mem[i], w_o_mem[i])
        zf = jnp.zeros((n_p, CONV_W - 1, 2 * D_FF), dt)
        fp, sfp = _conv_ffn(_rmsnorm(yp, g_ffn[i]), zf, w_up[i], conv_ffn[i], w_down[i])
        fs, sfs = _conv_ffn(_rmsnorm(ys, g_ffn[i]), state_ffn_conv[i], w_up[i], conv_ffn[i], w_down[i])
        ffn_p.append(sfp)
        ffn_s.append(sfs)
        yp = yp + fp
        ys = ys + fs
    y_prompt = _rmsnorm(yp, g_final)
    y_sample = _rmsnorm(ys, g_final)
    p_win0, p_win1, p_win2 = jnp.stack(win_p[0]), jnp.stack(win_p[1]), jnp.stack(win_p[2])
    s_win0, s_win1, s_win2 = jnp.stack(win_s[0]), jnp.stack(win_s[1]), jnp.stack(win_s[2])
    p_conv_b, s_conv_b = jnp.stack(conv_p), jnp.stack(conv_s)
    p_ffn_conv, s_ffn_conv = jnp.stack(ffn_p), jnp.stack(ffn_s)
    p_mem_kv = jnp.stack(mem_p)
    return (y_prompt, y_sample, p_win0, p_win1, p_win2, p_conv_b, p_ffn_conv, p_mem_kv,
            s_win0, s_win1, s_win2, s_conv_b, s_ffn_conv)
```

```python
import functools

import jax
import jax.numpy as jnp
from jax import lax
from jax.experimental import pallas as pl
from jax.experimental.pallas import tpu as pltpu

F32 = jnp.float32
BF16 = jnp.bfloat16

WINDOWS = (128, 512, 2048)
DILATIONS = (1, 4, 16)
N_GROUPS = 3
H_G = 4
HD_A = 128
BLOCK = 128
CONV_W = 3
H_M = 4
EPS = 1e-6
NEG = -1e30

LANE = 128
SUBLANE = 8
VMEM_LIMIT = 56 * 1024 * 1024


def _params(*sem):
    return pltpu.CompilerParams(dimension_semantics=sem, vmem_limit_bytes=VMEM_LIMIT)


def _norm_kernel(x_ref, g_ref, o_ref):
    x = x_ref[...]
    ms = jnp.mean(x * x, axis=-1, keepdims=True)
    o_ref[...] = (x * lax.rsqrt(ms + EPS) * g_ref[...]).astype(o_ref.dtype)


def _rmsnorm(x, g, out_dtype):
    m, d = x.shape
    bm = min(m, 512)
    return pl.pallas_call(
        _norm_kernel,
        grid=(m // bm,),
        in_specs=[pl.BlockSpec((bm, d), lambda i: (i, 0)),
                  pl.BlockSpec((1, d), lambda i: (0, 0))],
        out_specs=pl.BlockSpec((bm, d), lambda i: (i, 0)),
        out_shape=jax.ShapeDtypeStruct((m, d), out_dtype),
        compiler_params=_params("parallel"),
        name="rmsnorm",
    )(x, g.reshape(1, d))


def _mm_kernel(*refs, has_res):
    if has_res:
        h_ref, w_ref, r_ref, o_ref, wb_ref = refs
    else:
        h_ref, w_ref, o_ref, wb_ref = refs

    @pl.when(pl.program_id(1) == 0)
    def _():
        wb_ref[...] = w_ref[...].astype(BF16)

    acc = jnp.dot(h_ref[...].astype(BF16), wb_ref[...], preferred_element_type=F32)
    if has_res:
        acc = acc + r_ref[...]
    o_ref[...] = acc.astype(o_ref.dtype)


def _matmul(h, w, layer, *, res=None, out_dtype=F32, bm=1024, bn=512):
    m, k = h.shape
    n = w.shape[-1]
    bm = min(bm, m)
    bn = min(bn, n)
    in_specs = [pl.BlockSpec((bm, k), lambda j, i: (i, 0)),
                pl.BlockSpec((None, k, bn), lambda j, i: (layer, 0, j))]
    args = [h, w]
    if res is not None:
        in_specs.append(pl.BlockSpec((bm, bn), lambda j, i: (i, j)))
        args.append(res)
    return pl.pallas_call(
        functools.partial(_mm_kernel, has_res=res is not None),
        grid=(n // bn, m // bm),
        in_specs=in_specs,
        out_specs=pl.BlockSpec((bm, bn), lambda j, i: (i, j)),
        out_shape=jax.ShapeDtypeStruct((m, n), out_dtype),
        scratch_shapes=[pltpu.VMEM((k, bn), BF16)],
        compiler_params=_params("arbitrary", "arbitrary"),
        name="matmul",
    )(*args)


def _attn_block(q, k, v, valid):
    s = lax.dot_general(q.astype(BF16), k.astype(BF16), (((1,), (1,)), ((), ())),
                        preferred_element_type=F32) * (HD_A ** -0.5)
    s = jnp.where(valid, s, NEG)
    m = jnp.max(s, axis=-1, keepdims=True)
    p = jnp.exp(s - m)
    l = jnp.sum(p, axis=-1, keepdims=True)
    acc = jnp.dot(p.astype(BF16), v.astype(BF16), preferred_element_type=F32)
    return acc, m, l


def _attn_prompt_kernel(*refs, seq):
    qkv = refs[:9]
    o_ref, acc_ref, m_ref, l_ref = refs[9:]

    qi = lax.broadcasted_iota(jnp.int32, (BLOCK, 2 * BLOCK), 0) + BLOCK
    ki = lax.broadcasted_iota(jnp.int32, (BLOCK, 2 * BLOCK), 1)
    band_valid = (qi - ki >= 0) & (qi - ki <= BLOCK)
    ci = lax.broadcasted_iota(jnp.int32, (BLOCK, BLOCK), 0)
    cj = lax.broadcasted_iota(jnp.int32, (BLOCK, BLOCK), 1)
    first_valid = ci >= cj

    for g in range(N_GROUPS):
        q_ref, k_ref, v_ref = qkv[3 * g:3 * g + 3]
        dil = DILATIONS[g]
        n_blocks = seq // dil // BLOCK
        for r in range(dil):
            for b in range(n_blocks):
                q_rows = pl.ds(r + dil * b * BLOCK, BLOCK, stride=dil)
                if b == 0:
                    k_rows, valid = q_rows, first_valid
                else:
                    k_rows = pl.ds(r + dil * (b - 1) * BLOCK, 2 * BLOCK, stride=dil)
                    valid = band_valid
                acc, m, l = _attn_block(q_ref[q_rows, :], k_ref[k_rows, :], v_ref[k_rows, :], valid)
                acc_ref[g, q_rows, :] = acc
                m_ref[g, q_rows, :] = jnp.broadcast_to(m, (BLOCK, LANE))
                l_ref[g, q_rows, :] = jnp.broadcast_to(l, (BLOCK, LANE))

    def merge(c, carry):
        rows = pl.ds(pl.multiple_of(c * BLOCK, BLOCK), BLOCK)
        ms = [m_ref[g, rows, :] for g in range(N_GROUPS)]
        m = jnp.maximum(jnp.maximum(ms[0], ms[1]), ms[2])
        num = jnp.zeros((BLOCK, HD_A), F32)
        den = jnp.zeros((BLOCK, LANE), F32)
        for g in range(N_GROUPS):
            e = jnp.exp(ms[g] - m)
            num = num + e * acc_ref[g, rows, :]
            den = den + e * l_ref[g, rows, :]
        o_ref[rows, :] = (num / den).astype(o_ref.dtype)
        return carry

    lax.fori_loop(0, seq // BLOCK, merge, 0)


def _attn_prompt(qkv, n_seq, seq):
    def col(g, c):
        return lambda n, h: (n, (g * 3 + c) * H_G + h)

    in_specs = [pl.BlockSpec((seq, HD_A), col(g, c)) for g in range(N_GROUPS) for c in range(3)]
    return pl.pallas_call(
        functools.partial(_attn_prompt_kernel, seq=seq),
        grid=(n_seq, H_G),
        in_specs=in_specs,
        out_specs=pl.BlockSpec((seq, HD_A), lambda n, h: (n, h)),
        out_shape=jax.ShapeDtypeStruct((n_seq * seq, H_G * HD_A), BF16),
        scratch_shapes=[pltpu.VMEM((N_GROUPS, seq, HD_A), F32),
                        pltpu.VMEM((N_GROUPS, seq, LANE), F32),
                        pltpu.VMEM((N_GROUPS, seq, LANE), F32)],
        compiler_params=_params("parallel", "parallel"),
        name="attn_prompt",
    )(*([qkv] * 9))


def _attn_sample_kernel(qkv_ref, c0_ref, c1_ref, c2_ref, o_ref, w0_ref, w1_ref, w2_ref, *, t_new):
    caches = (c0_ref, c1_ref, c2_ref)
    wins = (w0_ref, w1_ref, w2_ref)
    kv_w = 2 * H_G * HD_A
    accs = [[] for _ in range(N_GROUPS)]
    ms = [[] for _ in range(N_GROUPS)]
    ls = [[] for _ in range(N_GROUPS)]
    for g in range(N_GROUPS):
        c_ref, w_ref = caches[g], wins[g]
        n_buf = c_ref.shape[0]
        dil = DILATIONS[g]
        n_back = WINDOWS[g] // dil
        base = g * 3 * H_G * HD_A
        w_ref[pl.ds(0, n_buf - t_new), :] = c_ref[pl.ds(t_new, n_buf - t_new), :]
        w_ref[pl.ds(n_buf - t_new, t_new), :] = qkv_ref[:, base + H_G * HD_A: base + H_G * HD_A + kv_w]

        t_w = lax.broadcasted_iota(jnp.int32, (t_new, n_buf), 0)
        c_w = lax.broadcasted_iota(jnp.int32, (t_new, n_buf), 1)
        diff_w = n_buf + t_w - (c_w + t_new)
        valid_w = (diff_w >= 0) & ((diff_w & (dil - 1)) == 0) & (diff_w <= dil * n_back)
        t_o = lax.broadcasted_iota(jnp.int32, (t_new, LANE), 0)
        c_o = lax.broadcasted_iota(jnp.int32, (t_new, LANE), 1)
        diff_o = n_buf + t_o - c_o
        valid_o = (c_o < t_new) & ((diff_o & (dil - 1)) == 0) & (diff_o <= dil * n_back)

        for h in range(H_G):
            q = qkv_ref[:, base + h * HD_A: base + (h + 1) * HD_A].astype(BF16)
            k_cols = slice(h * HD_A, (h + 1) * HD_A)
            v_cols = slice(H_G * HD_A + h * HD_A, H_G * HD_A + (h + 1) * HD_A)
            dn = (((1,), (1,)), ((), ()))
            s_w = lax.dot_general(q, w_ref[:, k_cols].astype(BF16), dn,
                                  preferred_element_type=F32) * (HD_A ** -0.5)
            s_o = lax.dot_general(q, c_ref[pl.ds(0, LANE), k_cols].astype(BF16), dn,
                                  preferred_element_type=F32) * (HD_A ** -0.5)
            s_w = jnp.where(valid_w, s_w, NEG)
            s_o = jnp.where(valid_o, s_o, NEG)
            m = jnp.maximum(jnp.max(s_w, axis=-1, keepdims=True), jnp.max(s_o, axis=-1, keepdims=True))
            p_w = jnp.exp(s_w - m)
            p_o = jnp.exp(s_o - m)
            l = jnp.sum(p_w, axis=-1, keepdims=True) + jnp.sum(p_o, axis=-1, keepdims=True)
            acc = (jnp.dot(p_w.astype(BF16), w_ref[:, v_cols].astype(BF16), preferred_element_type=F32)
                   + jnp.dot(p_o.astype(BF16), c_ref[pl.ds(0, LANE), v_cols].astype(BF16),
                             preferred_element_type=F32))
            accs[g].append(acc)
            ms[g].append(m)
            ls[g].append(l)

    for h in range(H_G):
        m = jnp.maximum(jnp.maximum(ms[0][h], ms[1][h]), ms[2][h])
        num = jnp.zeros((t_new, HD_A), F32)
        den = jnp.zeros((t_new, 1), F32)
        for g in range(N_GROUPS):
            e = jnp.exp(ms[g][h] - m)
            num = num + e * accs[g][h]
            den = den + e * ls[g][h]
        o_ref[:, h * HD_A:(h + 1) * HD_A] = num / den


def _attn_sample(qkv, caches, layer, n_seq, t_new):
    kv_w = 2 * H_G * HD_A
    in_specs = [pl.BlockSpec((t_new, qkv.shape[1]), lambda n: (n, 0))]
    out_specs = [pl.BlockSpec((t_new, H_G * HD_A), lambda n: (n, 0))]
    out_shape = [jax.ShapeDtypeStruct((n_seq * t_new, H_G * HD_A), F32)]
    for c in caches:
        n_buf = c.shape[2]
        in_specs.append(pl.BlockSpec((None, None, n_buf, kv_w), lambda n: (layer, n, 0, 0)))
        out_specs.append(pl.BlockSpec((None, n_buf, kv_w), lambda n: (n, 0, 0)))
        out_shape.append(jax.ShapeDtypeStruct((n_seq, n_buf, kv_w), F32))
    return pl.pallas_call(
        functools.partial(_attn_sample_kernel, t_new=t_new),
        grid=(n_seq,),
        in_specs=in_specs,
        out_specs=out_specs,
        out_shape=out_shape,
        compiler_params=_params("parallel"),
        name="attn_sample",
    )(qkv, *caches)


def _conv_chunks(rows):
    step = min(rows, 256)
    return [(s, step) for s in range(0, rows, step)]


def _fill_ext(ext_ref, prev, u_ref):
    rows = u_ref.shape[0]
    ext_ref[pl.ds(SUBLANE - 2, 2), :] = prev
    ext_ref[pl.ds(SUBLANE, rows), :] = u_ref[...]


def _conv3(ext_ref, w, start, size):
    out = w[0:1, :] * ext_ref[pl.ds(SUBLANE - 2 + start, size), :]
    out = out + w[1:2, :] * ext_ref[pl.ds(SUBLANE - 1 + start, size), :]
    return out + w[2:3, :] * ext_ref[pl.ds(SUBLANE + start, size), :]


def _gate_b_kernel(b_ref, c_ref, v_ref, prev_ref, w_ref, o_ref, st_ref, ext_ref):
    rows = b_ref.shape[0]
    ext_ref[pl.ds(SUBLANE - 2, 2), :] = prev_ref[...]
    for s, n in _conv_chunks(rows):
        ext_ref[pl.ds(SUBLANE + s, n), :] = c_ref[pl.ds(s, n), :] * v_ref[pl.ds(s, n), :]
    w = w_ref[...]
    for s, n in _conv_chunks(rows):
        o_ref[pl.ds(s, n), :] = (b_ref[pl.ds(s, n), :] * _conv3(ext_ref, w, s, n)).astype(o_ref.dtype)
    st_ref[...] = ext_ref[pl.ds(SUBLANE + rows - 2, 2), :]


def _gate_b(u, prev, w, layer, n_seq, rows, out_dtype, bc=512):
    d = u.shape[1] // 3
    nc = d // bc
    return pl.pallas_call(
        _gate_b_kernel,
        grid=(n_seq, nc),
        in_specs=[pl.BlockSpec((rows, bc), lambda n, c: (n, c)),
                  pl.BlockSpec((rows, bc), lambda n, c: (n, nc + c)),
                  pl.BlockSpec((rows, bc), lambda n, c: (n, 2 * nc + c)),
                  pl.BlockSpec((None, CONV_W - 1, bc), lambda n, c: (n, 0, c)),
                  pl.BlockSpec((None, CONV_W, bc), lambda n, c: (layer, 0, c))],
        out_specs=[pl.BlockSpec((rows, bc), lambda n, c: (n, c)),
                   pl.BlockSpec((None, CONV_W - 1, bc), lambda n, c: (n, 0, c))],
        out_shape=[jax.ShapeDtypeStruct((n_seq * rows, d), out_dtype),
                   jax.ShapeDtypeStruct((n_seq, CONV_W - 1, d), F32)],
        scratch_shapes=[pltpu.VMEM((rows + SUBLANE, bc), F32)],
        compiler_params=_params("parallel", "parallel"),
        name="gate_b",
    )(u, u, u, prev, w)


def _ffn_gate_kernel(a_ref, b_ref, pa_ref, pb_ref, wa_ref, wb_ref, o_ref, ea_ref, eb_ref):
    rows = a_ref.shape[0]
    _fill_ext(ea_ref, pa_ref[...], a_ref)
    _fill_ext(eb_ref, pb_ref[...], b_ref)
    wa = wa_ref[...]
    wb = wb_ref[...]
    for s, n in _conv_chunks(rows):
        ca = _conv3(ea_ref, wa, s, n)
        cb = _conv3(eb_ref, wb, s, n)
        silu = ca * (1.0 / (1.0 + jnp.exp(-ca)))
        o_ref[pl.ds(s, n), :] = (silu * cb).astype(o_ref.dtype)


def _ffn_gate(u, prev, w, layer, n_seq, rows, out_dtype, bc=512):
    f = u.shape[1] // 2
    nc = f // bc
    return pl.pallas_call(
        _ffn_gate_kernel,
        grid=(n_seq, nc),
        in_specs=[pl.BlockSpec((rows, bc), lambda n, c: (n, c)),
                  pl.BlockSpec((rows, bc), lambda n, c: (n, nc + c)),
                  pl.BlockSpec((None, CONV_W - 1, bc), lambda n, c: (n, 0, c)),
                  pl.BlockSpec((None, CONV_W - 1, bc), lambda n, c: (n, 0, nc + c)),
                  pl.BlockSpec((None, CONV_W, bc), lambda n, c: (layer, 0, c)),
                  pl.BlockSpec((None, CONV_W, bc), lambda n, c: (layer, 0, nc + c))],
        out_specs=pl.BlockSpec((rows, bc), lambda n, c: (n, c)),
        out_shape=jax.ShapeDtypeStruct((n_seq * rows, f), out_dtype),
        scratch_shapes=[pltpu.VMEM((rows + SUBLANE, bc), F32),
                        pltpu.VMEM((rows + SUBLANE, bc), F32)],
        compiler_params=_params("parallel", "parallel"),
        name="ffn_gate",
    )(u, u, prev, prev, w, w)


def _mem_attn_kernel(q_ref, k_ref, v_ref, o_ref):
    rows, hd = q_ref.shape
    kb = k_ref[...].astype(BF16)
    vb = v_ref[...].astype(BF16)
    chunk = min(rows, 256)

    def body(c, carry):
        r = pl.ds(pl.multiple_of(c * chunk, chunk), chunk)
        s = lax.dot_general(q_ref[r, :].astype(BF16), kb, (((1,), (1,)), ((), ())),
                            preferred_element_type=F32) * (hd ** -0.5)
        m = jnp.max(s, axis=-1, keepdims=True)
        p = jnp.exp(s - m)
        l = jnp.sum(p, axis=-1, keepdims=True)
        o = jnp.dot(p.astype(BF16), vb, preferred_element_type=F32)
        o_ref[r, :] = (o / l).astype(o_ref.dtype)
        return carry

    lax.fori_loop(0, rows // chunk, body, 0)


def _mem_attn(q, kv, layer, n_seq, rows, out_dtype):
    hd = q.shape[1] // H_M
    n_mem = kv.shape[-2]
    if kv.ndim == 4:
        k_spec = pl.BlockSpec((None, None, n_mem, hd), lambda n, h: (layer, n, 0, h))
        v_spec = pl.BlockSpec((None, None, n_mem, hd), lambda n, h: (layer, n, 0, H_M + h))
    else:
        k_spec = pl.BlockSpec((None, n_mem, hd), lambda n, h: (n, 0, h))
        v_spec = pl.BlockSpec((None, n_mem, hd), lambda n, h: (n, 0, H_M + h))
    return pl.pallas_call(
        _mem_attn_kernel,
        grid=(n_seq, H_M),
        in_specs=[pl.BlockSpec((rows, hd), lambda n, h: (n, h)), k_spec, v_spec],
        out_specs=pl.BlockSpec((rows, hd), lambda n, h: (n, h)),
        out_shape=jax.ShapeDtypeStruct(q.shape, out_dtype),
        compiler_params=_params("parallel", "parallel"),
        name="mem_attn",
    )(q, kv, kv)


def kernel(x_prompt, x_sample, mem_prompt, cache_win0_kv, cache_win1_kv, cache_win2_kv, state_conv_b, state_ffn_conv, cache_mem_kv, g_mix, g_mem_q, g_mem_kv, g_ffn, g_final, w_in_a, w_out_a, w_in_b, conv_b, w_out_b, w_q_mem, w_kv_mem, w_o_mem, w_up, conv_ffn, w_down):
    n_p, seq, d = x_prompt.shape
    n_s, t_new, _ = x_sample.shape
    depth = g_mix.shape[0]
    n_mem = mem_prompt.shape[1]
    ff2 = w_up.shape[-1]
    kv_w = 2 * H_G * HD_A

    yp = x_prompt.reshape(n_p * seq, d)
    ys = x_sample.reshape(n_s * t_new, d)
    mem = mem_prompt.reshape(n_p * n_mem, d)
    caches = [c.reshape(c.shape[0], c.shape[1], c.shape[2], kv_w)
              for c in (cache_win0_kv, cache_win1_kv, cache_win2_kv)]
    cache_mem = cache_mem_kv.reshape(depth, n_s, n_mem, 2 * d)
    zeros_b = jnp.zeros((n_p, CONV_W - 1, d), F32)
    zeros_f = jnp.zeros((n_p, CONV_W - 1, ff2), F32)

    win_p = [[] for _ in range(N_GROUPS)]
    win_s = [[] for _ in range(N_GROUPS)]
    conv_p, conv_s, ffn_p, ffn_s, mem_p = [], [], [], [], []

    for i in range(depth):
        j = i // 2
        hp = _rmsnorm(yp, g_mix[i], BF16)
        hs = _rmsnorm(ys, g_mix[i], BF16)
        if i % 2 == 0:
            qkv_p = _matmul(hp, w_in_a, j)
            qkv_s = _matmul(hs, w_in_a, j)
            o_p = _attn_prompt(qkv_p, n_p, seq)
            o_s, *new_wins = _attn_sample(qkv_s, caches, j, n_s, t_new)
            yp = _matmul(o_p, w_out_a, j, res=yp)
            ys = _matmul(o_s, w_out_a, j, res=ys)
            qkv_r = qkv_p.reshape(n_p, seq, N_GROUPS, 3, H_G, HD_A)
            for g in range(N_GROUPS):
                keep = min(WINDOWS[g], seq)
                win_p[g].append(qkv_r[:, seq - keep:, g, 1:])
                win_s[g].append(new_wins[g].reshape(n_s, -1, 2, H_G, HD_A))
        else:
            u_p = _matmul(hp, w_in_b, j)
            u_s = _matmul(hs, w_in_b, j)
            o_p, st_p = _gate_b(u_p, zeros_b, conv_b, j, n_p, seq, BF16)
            o_s, st_s = _gate_b(u_s, state_conv_b[j], conv_b, j, n_s, t_new, F32)
            yp = _matmul(o_p, w_out_b, j, res=yp)
            ys = _matmul(o_s, w_out_b, j, res=ys)
            conv_p.append(st_p)
            conv_s.append(st_s)

        kv_p = _matmul(_rmsnorm(mem, g_mem_kv[i], BF16), w_kv_mem, i)
        mem_p.append(kv_p.reshape(n_p, n_mem, 2, H_M, d // H_M))
        q_p = _matmul(_rmsnorm(yp, g_mem_q[i], BF16), w_q_mem, i, out_dtype=BF16)
        q_s = _matmul(_rmsnorm(ys, g_mem_q[i], BF16), w_q_mem, i)
        a_p = _mem_attn(q_p, kv_p.reshape(n_p, n_mem, 2 * d), 0, n_p, seq, BF16)
        a_s = _mem_attn(q_s, cache_mem, i, n_s, t_new, F32)
        yp = _matmul(a_p, w_o_mem, i, res=yp)
        ys = _matmul(a_s, w_o_mem, i, res=ys)

        u_p = _matmul(_rmsnorm(yp, g_ffn[i], BF16), w_up, i)
        u_s = _matmul(_rmsnorm(ys, g_ffn[i], BF16), w_up, i)
        f_p = _ffn_gate(u_p, zeros_f, conv_ffn, i, n_p, seq, BF16)
        f_s = _ffn_gate(u_s, state_ffn_conv[i], conv_ffn, i, n_s, t_new, F32)
        ffn_p.append(u_p.reshape(n_p, seq, ff2)[:, seq - (CONV_W - 1):])
        ffn_s.append(u_s.reshape(n_s, t_new, ff2)[:, t_new - (CONV_W - 1):])
        yp = _matmul(f_p, w_down, i, res=yp, bm=512)
        ys = _matmul(f_s, w_down, i, res=ys, bm=512)

    y_prompt = _rmsnorm(yp, g_final, F32).reshape(n_p, seq, d)
    y_sample = _rmsnorm(ys, g_final, F32).reshape(n_s, t_new, d)
    return (y_prompt, y_sample,
            jnp.stack(win_p[0]), jnp.stack(win_p[1]), jnp.stack(win_p[2]),
            jnp.stack(conv_p), jnp.stack(ffn_p), jnp.stack(mem_p),
            jnp.stack(win_s[0]), jnp.stack(win_s[1]), jnp.stack(win_s[2]),
            jnp.stack(conv_s), jnp.stack(ffn_s))
```

```python
import functools

import jax
import jax.numpy as jnp
from jax import lax
from jax.experimental import pallas as pl
from jax.experimental.pallas import tpu as pltpu

F32 = jnp.float32
BF16 = jnp.bfloat16

WINDOWS = (128, 512, 2048)
DILATIONS = (1, 4, 16)
N_GROUPS = 3
H_G = 4
HD_A = 128
BLOCK = 128
CONV_W = 3
H_M = 4
EPS = 1e-6
NEG = -1e30

LANE = 128
SUBLANE = 8
VMEM_LIMIT = 56 * 1024 * 1024
KV_ROWS = 2 * H_G


def _params(*sem):
    return pltpu.CompilerParams(dimension_semantics=sem, vmem_limit_bytes=VMEM_LIMIT)


def _dot(a, b):
    return jnp.dot(a, b, preferred_element_type=F32)


def _dot_nt(a, b):
    return lax.dot_general(a, b, (((1,), (1,)), ((), ())), preferred_element_type=F32)


def _norm_kernel(x_ref, g_ref, o_ref):
    x = x_ref[...]
    ms = jnp.mean(x * x, axis=-1, keepdims=True)
    o_ref[...] = (x * lax.rsqrt(ms + EPS) * g_ref[...]).astype(o_ref.dtype)


def _rmsnorm(x, g, out_dtype):
    m, d = x.shape
    bm = min(m, 512)
    return pl.pallas_call(
        _norm_kernel,
        grid=(m // bm,),
        in_specs=[pl.BlockSpec((bm, d), lambda i: (i, 0)),
                  pl.BlockSpec((1, d), lambda i: (0, 0))],
        out_specs=pl.BlockSpec((bm, d), lambda i: (i, 0)),
        out_shape=jax.ShapeDtypeStruct((m, d), out_dtype),
        compiler_params=_params("parallel"),
        name="rmsnorm",
    )(x, g.reshape(1, d))


def _proj_kernel(*refs, has_res):
    if has_res:
        hp_ref, hs_ref, w_ref, rp_ref, rs_ref, op_ref, os_ref, wb_ref = refs
    else:
        hp_ref, hs_ref, w_ref, op_ref, os_ref, wb_ref = refs

    @pl.when(pl.program_id(1) == 0)
    def _():
        wb_ref[...] = w_ref[...].astype(BF16)
        acc_s = _dot(hs_ref[...].astype(BF16), wb_ref[...])
        if has_res:
            acc_s = acc_s + rs_ref[...]
        os_ref[...] = acc_s.astype(os_ref.dtype)

    acc = _dot(hp_ref[...].astype(BF16), wb_ref[...])
    if has_res:
        acc = acc + rp_ref[...]
    op_ref[...] = acc.astype(op_ref.dtype)


def _proj(hp, hs, w, layer, *, res=None, out_dtype=F32, bm=1024, bn=512):
    mp, k = hp.shape
    ms = hs.shape[0]
    n = w.shape[-1]
    in_specs = [pl.BlockSpec((bm, k), lambda j, i: (i, 0)),
                pl.BlockSpec((ms, k), lambda j, i: (0, 0)),
                pl.BlockSpec((None, k, bn), lambda j, i: (layer, 0, j))]
    args = [hp, hs, w]
    if res is not None:
        in_specs += [pl.BlockSpec((bm, bn), lambda j, i: (i, j)),
                     pl.BlockSpec((ms, bn), lambda j, i: (0, j))]
        args += list(res)
    return pl.pallas_call(
        functools.partial(_proj_kernel, has_res=res is not None),
        grid=(n // bn, mp // bm),
        in_specs=in_specs,
        out_specs=[pl.BlockSpec((bm, bn), lambda j, i: (i, j)),
                   pl.BlockSpec((ms, bn), lambda j, i: (0, j))],
        out_shape=[jax.ShapeDtypeStruct((mp, n), out_dtype),
                   jax.ShapeDtypeStruct((ms, n), F32)],
        scratch_shapes=[pltpu.VMEM((k, bn), BF16)],
        compiler_params=_params("arbitrary", "arbitrary"),
        name="proj",
    )(*args)


def _matmul(h, w, layer, *, bm, bn):
    m, k = h.shape
    n = w.shape[-1]

    def body(h_ref, w_ref, o_ref, wb_ref):
        @pl.when(pl.program_id(1) == 0)
        def _():
            wb_ref[...] = w_ref[...].astype(BF16)
        o_ref[...] = _dot(h_ref[...], wb_ref[...])

    return pl.pallas_call(
        body,
        grid=(n // bn, m // bm),
        in_specs=[pl.BlockSpec((bm, k), lambda j, i: (i, 0)),
                  pl.BlockSpec((None, k, bn), lambda j, i: (layer, 0, j))],
        out_specs=pl.BlockSpec((bm, bn), lambda j, i: (i, j)),
        out_shape=jax.ShapeDtypeStruct((m, n), F32),
        scratch_shapes=[pltpu.VMEM((k, bn), BF16)],
        compiler_params=_params("arbitrary", "arbitrary"),
        name="matmul",
    )(h, w)


def _conv_chunks(rows):
    step = min(rows, 256)
    return [(s, step) for s in range(0, rows, step)]


def _conv3(ext_ref, w, start, size):
    out = w[0:1, :] * ext_ref[pl.ds(SUBLANE - 2 + start, size), :]
    out = out + w[1:2, :] * ext_ref[pl.ds(SUBLANE - 1 + start, size), :]
    return out + w[2:3, :] * ext_ref[pl.ds(SUBLANE + start, size), :]


def _carry_in(ext_ref, first_of_seq, bm):
    @pl.when(first_of_seq)
    def _():
        ext_ref[pl.ds(SUBLANE - 2, 2), :] = jnp.zeros((2, ext_ref.shape[1]), F32)

    @pl.when(jnp.logical_not(first_of_seq))
    def _():
        ext_ref[pl.ds(SUBLANE - 2, 2), :] = ext_ref[pl.ds(SUBLANE + bm - 2, 2), :]


def _silu(x):
    return x * (1.0 / (1.0 + jnp.exp(-x)))


def _ffn_up_kernel(hp_ref, hs_ref, wa_ref, wb_ref, cwa_ref, cwb_ref, psa_ref, psb_ref,
                   op_ref, os_ref, stpa_ref, stpb_ref, stsa_ref, stsb_ref,
                   w_scr, ea_ref, eb_ref, sa_ref, sb_ref, *, blocks_per_seq, n_s, t_new):
    i = pl.program_id(1)
    bm = hp_ref.shape[0]
    bn = wa_ref.shape[1]
    cwa = cwa_ref[...]
    cwb = cwb_ref[...]
    ext_s = t_new + SUBLANE

    @pl.when(i == 0)
    def _():
        w_scr[:, pl.ds(0, bn)] = wa_ref[...].astype(BF16)
        w_scr[:, pl.ds(bn, bn)] = wb_ref[...].astype(BF16)
        hs = hs_ref[...].astype(BF16)
        ua = _dot(hs, w_scr[:, pl.ds(0, bn)])
        ub = _dot(hs, w_scr[:, pl.ds(bn, bn)])
        for n in range(n_s):
            base = n * ext_s
            sa_ref[pl.ds(base + SUBLANE - 2, 2), :] = psa_ref[n]
            sb_ref[pl.ds(base + SUBLANE - 2, 2), :] = psb_ref[n]
            sa_ref[pl.ds(base + SUBLANE, t_new), :] = ua[n * t_new:(n + 1) * t_new]
            sb_ref[pl.ds(base + SUBLANE, t_new), :] = ub[n * t_new:(n + 1) * t_new]
        for n in range(n_s):
            base = n * ext_s
            ca = _conv3(sa_ref, cwa, base, t_new)
            cb = _conv3(sb_ref, cwb, base, t_new)
            os_ref[pl.ds(n * t_new, t_new), :] = (_silu(ca) * cb).astype(os_ref.dtype)
            stsa_ref[n] = sa_ref[pl.ds(base + SUBLANE + t_new - 2, 2), :]
            stsb_ref[n] = sb_ref[pl.ds(base + SUBLANE + t_new - 2, 2), :]

    first = (i % blocks_per_seq) == 0
    _carry_in(ea_ref, first, bm)
    _carry_in(eb_ref, first, bm)
    hp = hp_ref[...]
    ea_ref[pl.ds(SUBLANE, bm), :] = _dot(hp, w_scr[:, pl.ds(0, bn)])
    eb_ref[pl.ds(SUBLANE, bm), :] = _dot(hp, w_scr[:, pl.ds(bn, bn)])
    for s, n in _conv_chunks(bm):
        ca = _conv3(ea_ref, cwa, s, n)
        cb = _conv3(eb_ref, cwb, s, n)
        op_ref[pl.ds(s, n), :] = (_silu(ca) * cb).astype(op_ref.dtype)

    @pl.when((i % blocks_per_seq) == blocks_per_seq - 1)
    def _():
        stpa_ref[...] = ea_ref[pl.ds(SUBLANE + bm - 2, 2), :]
        stpb_ref[...] = eb_ref[pl.ds(SUBLANE + bm - 2, 2), :]


def _ffn_up(hp, hs, w, conv_w, prev_s, layer, n_p, seq, n_s, t_new, *, bm=1024, bn=512):
    mp, k = hp.shape
    ms = hs.shape[0]
    f = w.shape[-1] // 2
    nc = f // bn
    bps = seq // bm
    st = CONV_W - 1
    outs = pl.pallas_call(
        functools.partial(_ffn_up_kernel, blocks_per_seq=bps, n_s=n_s, t_new=t_new),
        grid=(nc, mp // bm),
        in_specs=[pl.BlockSpec((bm, k), lambda j, i: (i, 0)),
                  pl.BlockSpec((ms, k), lambda j, i: (0, 0)),
                  pl.BlockSpec((None, k, bn), lambda j, i: (layer, 0, j)),
                  pl.BlockSpec((None, k, bn), lambda j, i: (layer, 0, nc + j)),
                  pl.BlockSpec((None, CONV_W, bn), lambda j, i: (layer, 0, j)),
                  pl.BlockSpec((None, CONV_W, bn), lambda j, i: (layer, 0, nc + j)),
                  pl.BlockSpec((None, n_s, st, bn), lambda j, i: (layer, 0, 0, j)),
                  pl.BlockSpec((None, n_s, st, bn), lambda j, i: (layer, 0, 0, nc + j))],
        out_specs=[pl.BlockSpec((bm, bn), lambda j, i: (i, j)),
                   pl.BlockSpec((ms, bn), lambda j, i: (0, j)),
                   pl.BlockSpec((None, st, bn), lambda j, i: (i // bps, 0, j)),
                   pl.BlockSpec((None, st, bn), lambda j, i: (i // bps, 0, j)),
                   pl.BlockSpec((n_s, st, bn), lambda j, i: (0, 0, j)),
                   pl.BlockSpec((n_s, st, bn), lambda j, i: (0, 0, j))],
        out_shape=[jax.ShapeDtypeStruct((mp, f), BF16),
                   jax.ShapeDtypeStruct((ms, f), BF16),
                   jax.ShapeDtypeStruct((n_p, st, f), F32),
                   jax.ShapeDtypeStruct((n_p, st, f), F32),
                   jax.ShapeDtypeStruct((n_s, st, f), F32),
                   jax.ShapeDtypeStruct((n_s, st, f), F32)],
        scratch_shapes=[pltpu.VMEM((k, 2 * bn), BF16),
                        pltpu.VMEM((bm + SUBLANE, bn), F32),
                        pltpu.VMEM((bm + SUBLANE, bn), F32),
                        pltpu.VMEM((n_s * (t_new + SUBLANE), bn), F32),
                        pltpu.VMEM((n_s * (t_new + SUBLANE), bn), F32)],
        compiler_params=_params("arbitrary", "arbitrary"),
        name="ffn_up",
    )(hp, hs, w, w, conv_w, conv_w, prev_s, prev_s)
    gp, gs, stpa, stpb, stsa, stsb = outs
    return gp, gs, jnp.concatenate([stpa, stpb], axis=-1), jnp.concatenate([stsa, stsb], axis=-1)


def _mix_b_kernel(hp_ref, hs_ref, wb_ref, wc_ref, wv_ref, cw_ref, ps_ref,
                  op_ref, os_ref, stp_ref, sts_ref,
                  w_scr, e_ref, s_ref, *, blocks_per_seq, n_s, t_new):
    i = pl.program_id(1)
    bm = hp_ref.shape[0]
    bn = wb_ref.shape[1]
    cw = cw_ref[...]
    ext_s = t_new + SUBLANE

    @pl.when(i == 0)
    def _():
        w_scr[:, pl.ds(0, bn)] = wb_ref[...].astype(BF16)
        w_scr[:, pl.ds(bn, bn)] = wc_ref[...].astype(BF16)
        w_scr[:, pl.ds(2 * bn, bn)] = wv_ref[...].astype(BF16)
        hs = hs_ref[...].astype(BF16)
        gb = _dot(hs, w_scr[:, pl.ds(0, bn)])
        cv = _dot(hs, w_scr[:, pl.ds(bn, bn)]) * _dot(hs, w_scr[:, pl.ds(2 * bn, bn)])
        for n in range(n_s):
            base = n * ext_s
            s_ref[pl.ds(base + SUBLANE - 2, 2), :] = ps_ref[n]
            s_ref[pl.ds(base + SUBLANE, t_new), :] = cv[n * t_new:(n + 1) * t_new]
        for n in range(n_s):
            base = n * ext_s
            c = _conv3(s_ref, cw, base, t_new)
            os_ref[pl.ds(n * t_new, t_new), :] = (gb[n * t_new:(n + 1) * t_new] * c).astype(os_ref.dtype)
            sts_ref[n] = s_ref[pl.ds(base + SUBLANE + t_new - 2, 2), :]

    _carry_in(e_ref, (i % blocks_per_seq) == 0, bm)
    hp = hp_ref[...]
    e_ref[pl.ds(SUBLANE, bm), :] = _dot(hp, w_scr[:, pl.ds(bn, bn)]) * _dot(hp, w_scr[:, pl.ds(2 * bn, bn)])
    gb = _dot(hp, w_scr[:, pl.ds(0, bn)])
    for s, n in _conv_chunks(bm):
        op_ref[pl.ds(s, n), :] = (gb[s:s + n] * _conv3(e_ref, cw, s, n)).astype(op_ref.dtype)

    @pl.when((i % blocks_per_seq) == blocks_per_seq - 1)
    def _():
        stp_ref[...] = e_ref[pl.ds(SUBLANE + bm - 2, 2), :]


def _mix_b(hp, hs, w, conv_w, prev_s, layer, n_p, seq, n_s, t_new, *, bm=512, bn=512):
    mp, k = hp.shape
    ms = hs.shape[0]
    d = w.shape[-1] // 3
    nc = d // bn
    bps = seq // bm
    st = CONV_W - 1
    return pl.pallas_call(
        functools.partial(_mix_b_kernel, blocks_per_seq=bps, n_s=n_s, t_new=t_new),
        grid=(nc, mp // bm),
        in_specs=[pl.BlockSpec((bm, k), lambda j, i: (i, 0)),
                  pl.BlockSpec((ms, k), lambda j, i: (0, 0)),
                  pl.BlockSpec((None, k, bn), lambda j, i: (layer, 0, j)),
                  pl.BlockSpec((None, k, bn), lambda j, i: (layer, 0, nc + j)),
                  pl.BlockSpec((None, k, bn), lambda j, i: (layer, 0, 2 * nc + j)),
                  pl.BlockSpec((None, CONV_W, bn), lambda j, i: (layer, 0, j)),
                  pl.BlockSpec((None, n_s, st, bn), lambda j, i: (layer, 0, 0, j))],
        out_specs=[pl.BlockSpec((bm, bn), lambda j, i: (i, j)),
                   pl.BlockSpec((ms, bn), lambda j, i: (0, j)),
                   pl.BlockSpec((None, st, bn), lambda j, i: (i // bps, 0, j)),
                   pl.BlockSpec((n_s, st, bn), lambda j, i: (0, 0, j))],
        out_shape=[jax.ShapeDtypeStruct((mp, d), BF16),
                   jax.ShapeDtypeStruct((ms, d), BF16),
                   jax.ShapeDtypeStruct((n_p, st, d), F32),
                   jax.ShapeDtypeStruct((n_s, st, d), F32)],
        scratch_shapes=[pltpu.VMEM((k, 3 * bn), BF16),
                        pltpu.VMEM((bm + SUBLANE, bn), F32),
                        pltpu.VMEM((n_s * (t_new + SUBLANE), bn), F32)],
        compiler_params=_params("arbitrary", "arbitrary"),
        name="mix_b",
    )(hp, hs, w, w, w, conv_w, prev_s)


def _attn_block(q, k, v, valid):
    s = _dot_nt(q.astype(BF16), k.astype(BF16)) * (HD_A ** -0.5)
    s = jnp.where(valid, s, NEG)
    m = jnp.max(s, axis=-1, keepdims=True)
    p = jnp.exp(s - m)
    l = jnp.sum(p, axis=-1, keepdims=True)
    acc = _dot(p.astype(BF16), v.astype(BF16))
    return acc, m, l


def _attn_prompt_kernel(*refs, seq):
    qkv = refs[:9]
    o_ref, acc_ref, m_ref, l_ref = refs[9:]

    qi = lax.broadcasted_iota(jnp.int32, (BLOCK, 2 * BLOCK), 0) + BLOCK
    ki = lax.broadcasted_iota(jnp.int32, (BLOCK, 2 * BLOCK), 1)
    band_valid = (qi - ki >= 0) & (qi - ki <= BLOCK)
    ci = lax.broadcasted_iota(jnp.int32, (BLOCK, BLOCK), 0)
    cj = lax.broadcasted_iota(jnp.int32, (BLOCK, BLOCK), 1)
    first_valid = ci >= cj

    for g in range(N_GROUPS):
        q_ref, k_ref, v_ref = qkv[3 * g:3 * g + 3]
        dil = DILATIONS[g]
        n_blocks = seq // dil // BLOCK
        for r in range(dil):
            for b in range(n_blocks):
                q_rows = pl.ds(r + dil * b * BLOCK, BLOCK, stride=dil)
                if b == 0:
                    k_rows, valid = q_rows, first_valid
                else:
                    k_rows = pl.ds(r + dil * (b - 1) * BLOCK, 2 * BLOCK, stride=dil)
                    valid = band_valid
                acc, m, l = _attn_block(q_ref[q_rows, :], k_ref[k_rows, :], v_ref[k_rows, :], valid)
                acc_ref[g, q_rows, :] = acc
                m_ref[g, q_rows, :] = jnp.broadcast_to(m, (BLOCK, LANE))
                l_ref[g, q_rows, :] = jnp.broadcast_to(l, (BLOCK, LANE))

    def merge(c, carry):
        rows = pl.ds(pl.multiple_of(c * BLOCK, BLOCK), BLOCK)
        ms = [m_ref[g, rows, :] for g in range(N_GROUPS)]
        m = jnp.maximum(jnp.maximum(ms[0], ms[1]), ms[2])
        num = jnp.zeros((BLOCK, HD_A), F32)
        den = jnp.zeros((BLOCK, LANE), F32)
        for g in range(N_GROUPS):
            e = jnp.exp(ms[g] - m)
            num = num + e * acc_ref[g, rows, :]
            den = den + e * l_ref[g, rows, :]
        o_ref[rows, :] = (num / den).astype(o_ref.dtype)
        return carry

    lax.fori_loop(0, seq // BLOCK, merge, 0)


def _attn_prompt(qkv, n_seq, seq):
    def col(g, c):
        return lambda n, h: (n, (g * 3 + c) * H_G + h)

    in_specs = [pl.BlockSpec((seq, HD_A), col(g, c)) for g in range(N_GROUPS) for c in range(3)]
    return pl.pallas_call(
        functools.partial(_attn_prompt_kernel, seq=seq),
        grid=(n_seq, H_G),
        in_specs=in_specs,
        out_specs=pl.BlockSpec((seq, HD_A), lambda n, h: (n, h)),
        out_shape=jax.ShapeDtypeStruct((n_seq * seq, H_G * HD_A), BF16),
        scratch_shapes=[pltpu.VMEM((N_GROUPS, seq, HD_A), F32),
                        pltpu.VMEM((N_GROUPS, seq, LANE), F32),
                        pltpu.VMEM((N_GROUPS, seq, LANE), F32)],
        compiler_params=_params("parallel", "parallel"),
        name="attn_prompt",
    )(*([qkv] * 9))


def _win_kernel(*refs):
    ins, outs = refs[:2 * N_GROUPS], refs[2 * N_GROUPS:]
    for g in range(N_GROUPS):
        keep = ins[2 * g].shape[0]
        for c in range(2):
            for h in range(H_G):
                outs[g][pl.ds(c * H_G + h, keep, stride=KV_ROWS), :] = ins[2 * g + c][:, h * HD_A:(h + 1) * HD_A]


def _prompt_windows(qkv, n_seq, seq):
    hw = H_G * HD_A
    in_specs, out_specs, out_shape = [], [], []
    for g in range(N_GROUPS):
        keep = min(WINDOWS[g], seq)
        per = seq // keep
        for c in (1, 2):
            in_specs.append(pl.BlockSpec((keep, hw), lambda n, g=g, c=c, per=per: (n * per + per - 1, g * 3 + c)))
        out_specs.append(pl.BlockSpec((None, keep * KV_ROWS, HD_A), lambda n: (n, 0, 0)))
        out_shape.append(jax.ShapeDtypeStruct((n_seq, keep * KV_ROWS, HD_A), F32))
    return pl.pallas_call(
        _win_kernel,
        grid=(n_seq,),
        in_specs=in_specs,
        out_specs=out_specs,
        out_shape=out_shape,
        compiler_params=_params("parallel"),
        name="prompt_windows",
    )(*([qkv] * (2 * N_GROUPS)))


def _attn_sample_kernel(qkv_ref, c0_ref, c1_ref, c2_ref, o_ref, w0_ref, w1_ref, w2_ref, *, t_new):
    caches = (c0_ref, c1_ref, c2_ref)
    wins = (w0_ref, w1_ref, w2_ref)
    hw = H_G * HD_A
    accs = [[] for _ in range(N_GROUPS)]
    ms = [[] for _ in range(N_GROUPS)]
    ls = [[] for _ in range(N_GROUPS)]
    for g in range(N_GROUPS):
        c_ref, w_ref = caches[g], wins[g]
        n_buf = c_ref.shape[0] // KV_ROWS
        dil = DILATIONS[g]
        n_back = WINDOWS[g] // dil
        base = g * 3 * hw
        keep_rows = (n_buf - t_new) * KV_ROWS
        w_ref[pl.ds(0, keep_rows), :] = c_ref[pl.ds(t_new * KV_ROWS, keep_rows), :]
        for c in range(2):
            for h in range(H_G):
                col = base + (1 + c) * hw + h * HD_A
                w_ref[pl.ds(keep_rows + c * H_G + h, t_new, stride=KV_ROWS), :] = qkv_ref[:, col:col + HD_A]

        t_w = lax.broadcasted_iota(jnp.int32, (t_new, n_buf), 0)
        c_w = lax.broadcasted_iota(jnp.int32, (t_new, n_buf), 1)
        diff_w = n_buf + t_w - (c_w + t_new)
        valid_w = (diff_w >= 0) & ((diff_w & (dil - 1)) == 0) & (diff_w <= dil * n_back)
        t_o = lax.broadcasted_iota(jnp.int32, (t_new, LANE), 0)
        c_o = lax.broadcasted_iota(jnp.int32, (t_new, LANE), 1)
        diff_o = n_buf + t_o - c_o
        valid_o = (c_o < t_new) & ((diff_o & (dil - 1)) == 0) & (diff_o <= dil * n_back)

        for h in range(H_G):
            q = qkv_ref[:, base + h * HD_A: base + (h + 1) * HD_A].astype(BF16)
            k_w = w_ref[pl.ds(h, n_buf, stride=KV_ROWS), :].astype(BF16)
            v_w = w_ref[pl.ds(H_G + h, n_buf, stride=KV_ROWS), :].astype(BF16)
            k_o = c_ref[pl.ds(h, LANE, stride=KV_ROWS), :].astype(BF16)
            v_o = c_ref[pl.ds(H_G + h, LANE, stride=KV_ROWS), :].astype(BF16)
            s_w = jnp.where(valid_w, _dot_nt(q, k_w) * (HD_A ** -0.5), NEG)
            s_o = jnp.where(valid_o, _dot_nt(q, k_o) * (HD_A ** -0.5), NEG)
            m = jnp.maximum(jnp.max(s_w, axis=-1, keepdims=True), jnp.max(s_o, axis=-1, keepdims=True))
            p_w = jnp.exp(s_w - m)
            p_o = jnp.exp(s_o - m)
            l = jnp.sum(p_w, axis=-1, keepdims=True) + jnp.sum(p_o, axis=-1, keepdims=True)
            acc = _dot(p_w.astype(BF16), v_w) + _dot(p_o.astype(BF16), v_o)
            accs[g].append(acc)
            ms[g].append(m)
            ls[g].append(l)

    for h in range(H_G):
        m = jnp.maximum(jnp.maximum(ms[0][h], ms[1][h]), ms[2][h])
        num = jnp.zeros((t_new, HD_A), F32)
        den = jnp.zeros((t_new, 1), F32)
        for g in range(N_GROUPS):
            e = jnp.exp(ms[g][h] - m)
            num = num + e * accs[g][h]
            den = den + e * ls[g][h]
        o_ref[:, h * HD_A:(h + 1) * HD_A] = num / den


def _attn_sample(qkv, caches, layer, n_seq, t_new):
    in_specs = [pl.BlockSpec((t_new, qkv.shape[1]), lambda n: (n, 0))]
    out_specs = [pl.BlockSpec((t_new, H_G * HD_A), lambda n: (n, 0))]
    out_shape = [jax.ShapeDtypeStruct((n_seq * t_new, H_G * HD_A), F32)]
    for c in caches:
        rows = c.shape[2]
        in_specs.append(pl.BlockSpec((None, None, rows, HD_A), lambda n: (layer, n, 0, 0)))
        out_specs.append(pl.BlockSpec((None, rows, HD_A), lambda n: (n, 0, 0)))
        out_shape.append(jax.ShapeDtypeStruct((n_seq, rows, HD_A), F32))
    return pl.pallas_call(
        functools.partial(_attn_sample_kernel, t_new=t_new),
        grid=(n_seq,),
        in_specs=in_specs,
        out_specs=out_specs,
        out_shape=out_shape,
        compiler_params=_params("parallel"),
        name="attn_sample",
    )(qkv, *caches)


def _softmax_pv(q, kb, vb, scale):
    s = _dot_nt(q, kb) * scale
    m = jnp.max(s, axis=-1, keepdims=True)
    p = jnp.exp(s - m)
    l = jnp.sum(p, axis=-1, keepdims=True)
    return _dot(p.astype(BF16), vb) / l


def _mem_attn_prompt_kernel(q_ref, k_ref, v_ref, o_ref, kv_ref):
    rows, hd = q_ref.shape
    h = pl.program_id(1)
    k = k_ref[...]
    v = v_ref[...]
    for hh in range(H_M):
        @pl.when(h == hh)
        def _(hh=hh):
            kv_ref[:, 0, hh, :] = k
            kv_ref[:, 1, hh, :] = v
    kb = k.astype(BF16)
    vb = v.astype(BF16)
    chunk = min(rows, 256)

    def body(c, carry):
        r = pl.ds(pl.multiple_of(c * chunk, chunk), chunk)
        o_ref[r, :] = _softmax_pv(q_ref[r, :], kb, vb, hd ** -0.5).astype(o_ref.dtype)
        return carry

    lax.fori_loop(0, rows // chunk, body, 0)


def _mem_attn_prompt(q, kv, n_seq, rows):
    hd = q.shape[1] // H_M
    n_mem = kv.shape[0] // n_seq
    return pl.pallas_call(
        _mem_attn_prompt_kernel,
        grid=(n_seq, H_M),
        in_specs=[pl.BlockSpec((rows, hd), lambda n, h: (n, h)),
                  pl.BlockSpec((n_mem, hd), lambda n, h: (n, h)),
                  pl.BlockSpec((n_mem, hd), lambda n, h: (n, H_M + h))],
        out_specs=[pl.BlockSpec((rows, hd), lambda n, h: (n, h)),
                   pl.BlockSpec((None, n_mem, 2, H_M, hd), lambda n, h: (n, 0, 0, 0, 0))],
        out_shape=[jax.ShapeDtypeStruct(q.shape, BF16),
                   jax.ShapeDtypeStruct((n_seq, n_mem, 2, H_M, hd), F32)],
        compiler_params=_params("parallel", "arbitrary"),
        name="mem_attn_prompt",
    )(q, kv, kv)


def _mem_attn_sample_kernel(q_ref, kv_ref, o_ref):
    hd = kv_ref.shape[-1]
    for h in range(H_M):
        kb = kv_ref[:, 0, h, :].astype(BF16)
        vb = kv_ref[:, 1, h, :].astype(BF16)
        q = q_ref[:, h * hd:(h + 1) * hd].astype(BF16)
        o_ref[:, h * hd:(h + 1) * hd] = _softmax_pv(q, kb, vb, hd ** -0.5)


def _mem_attn_sample(q, cache, layer, n_seq, rows):
    n_mem, _, _, hd = cache.shape[2:]
    return pl.pallas_call(
        _mem_attn_sample_kernel,
        grid=(n_seq,),
        in_specs=[pl.BlockSpec((rows, q.shape[1]), lambda n: (n, 0)),
                  pl.BlockSpec((None, None, n_mem, 2, H_M, hd), lambda n: (layer, n, 0, 0, 0, 0))],
        out_specs=pl.BlockSpec((rows, q.shape[1]), lambda n: (n, 0)),
        out_shape=jax.ShapeDtypeStruct(q.shape, F32),
        compiler_params=_params("parallel"),
        name="mem_attn_sample",
    )(q, cache)


def kernel(x_prompt, x_sample, mem_prompt, cache_win0_kv, cache_win1_kv, cache_win2_kv, state_conv_b, state_ffn_conv, cache_mem_kv, g_mix, g_mem_q, g_mem_kv, g_ffn, g_final, w_in_a, w_out_a, w_in_b, conv_b, w_out_b, w_q_mem, w_kv_mem, w_o_mem, w_up, conv_ffn, w_down):
    n_p, seq, d = x_prompt.shape
    n_s, t_new, _ = x_sample.shape
    depth = g_mix.shape[0]
    n_mem = mem_prompt.shape[1]

    yp = x_prompt.reshape(n_p * seq, d)
    ys = x_sample.reshape(n_s * t_new, d)
    mem = mem_prompt.reshape(n_p * n_mem, d)
    caches = [c.reshape(c.shape[0], c.shape[1], c.shape[2] * KV_ROWS, HD_A)
              for c in (cache_win0_kv, cache_win1_kv, cache_win2_kv)]

    win_p = [[] for _ in range(N_GROUPS)]
    win_s = [[] for _ in range(N_GROUPS)]
    conv_p, conv_s, ffn_p, ffn_s, mem_p = [], [], [], [], []

    for i in range(depth):
        j = i // 2
        hp = _rmsnorm(yp, g_mix[i], BF16)
        hs = _rmsnorm(ys, g_mix[i], BF16)
        if i % 2 == 0:
            qkv_p, qkv_s = _proj(hp, hs, w_in_a, j, bm=512, bn=3 * H_G * HD_A)
            o_p = _attn_prompt(qkv_p, n_p, seq)
            o_s, *new_wins = _attn_sample(qkv_s, caches, j, n_s, t_new)
            yp, ys = _proj(o_p, o_s, w_out_a, j, res=(yp, ys))
            wins = _prompt_windows(qkv_p, n_p, seq)
            for g in range(N_GROUPS):
                win_p[g].append(wins[g].reshape(n_p, -1, 2, H_G, HD_A))
                win_s[g].append(new_wins[g].reshape(n_s, -1, 2, H_G, HD_A))
        else:
            o_p, o_s, st_p, st_s = _mix_b(hp, hs, w_in_b, conv_b, state_conv_b, j, n_p, seq, n_s, t_new)
            yp, ys = _proj(o_p, o_s, w_out_b, j, res=(yp, ys))
            conv_p.append(st_p)
            conv_s.append(st_s)

        kv_p = _matmul(_rmsnorm(mem, g_mem_kv[i], BF16), w_kv_mem, i, bm=n_p * n_mem, bn=512)
        q_p, q_s = _proj(_rmsnorm(yp, g_mem_q[i], BF16), _rmsnorm(ys, g_mem_q[i], BF16), w_q_mem, i,
                         out_dtype=BF16)
        a_p, kv_out = _mem_attn_prompt(q_p, kv_p, n_p, seq)
        mem_p.append(kv_out)
        a_s = _mem_attn_sample(q_s, cache_mem_kv, i, n_s, t_new)
        yp, ys = _proj(a_p, a_s, w_o_mem, i, res=(yp, ys))

        f_p, f_s, sf_p, sf_s = _ffn_up(_rmsnorm(yp, g_ffn[i], BF16), _rmsnorm(ys, g_ffn[i], BF16),
                                       w_up, conv_ffn, state_ffn_conv, i, n_p, seq, n_s, t_new)
        ffn_p.append(sf_p)
        ffn_s.append(sf_s)
        yp, ys = _proj(f_p, f_s, w_down, i, res=(yp, ys), bm=512)

    y_prompt = _rmsnorm(yp, g_final, F32).reshape(n_p, seq, d)
    y_sample = _rmsnorm(ys, g_final, F32).reshape(n_s, t_new, d)
    return (y_prompt, y_sample,
            jnp.stack(win_p[0]), jnp.stack(win_p[1]), jnp.stack(win_p[2]),
            jnp.stack(conv_p), jnp.stack(ffn_p), jnp.stack(mem_p),
            jnp.stack(win_s[0]), jnp.stack(win_s[1]), jnp.stack(win_s[2]),
            jnp.stack(conv_s), jnp.stack(ffn_s))
```

```python
import functools

import jax
import jax.numpy as jnp
from jax import lax
from jax.experimental import pallas as pl
from jax.experimental.pallas import tpu as pltpu

F32 = jnp.float32
BF16 = jnp.bfloat16

WINDOWS = (128, 512, 2048)
DILATIONS = (1, 4, 16)
N_GROUPS = 3
H_G = 4
HD_A = 128
BLOCK = 128
CONV_W = 3
H_M = 4
EPS = 1e-6
NEG = -1e30

LANE = 128
SUBLANE = 8
VMEM_LIMIT = 56 * 1024 * 1024
KV_ROWS = 2 * H_G


def _params(*sem):
    return pltpu.CompilerParams(dimension_semantics=sem, vmem_limit_bytes=VMEM_LIMIT)


def _dot(a, b):
    return jnp.dot(a, b, preferred_element_type=F32)


def _dot_nt(a, b):
    return lax.dot_general(a, b, (((1,), (1,)), ((), ())), preferred_element_type=F32)


def _norm_kernel(x_ref, g_ref, o_ref):
    x = x_ref[...]
    ms = jnp.mean(x * x, axis=-1, keepdims=True)
    o_ref[...] = (x * lax.rsqrt(ms + EPS) * g_ref[...]).astype(o_ref.dtype)


def _rmsnorm(x, g, out_dtype):
    m, d = x.shape
    bm = min(m, 512)
    return pl.pallas_call(
        _norm_kernel,
        grid=(m // bm,),
        in_specs=[pl.BlockSpec((bm, d), lambda i: (i, 0)),
                  pl.BlockSpec((1, d), lambda i: (0, 0))],
        out_specs=pl.BlockSpec((bm, d), lambda i: (i, 0)),
        out_shape=jax.ShapeDtypeStruct((m, d), out_dtype),
        compiler_params=_params("parallel"),
        name="rmsnorm",
    )(x, g.reshape(1, d))


def _proj_kernel(*refs, has_res):
    if has_res:
        hp_ref, hs_ref, w_ref, rp_ref, rs_ref, op_ref, os_ref, wb_ref = refs
    else:
        hp_ref, hs_ref, w_ref, op_ref, os_ref, wb_ref = refs

    @pl.when(pl.program_id(1) == 0)
    def _():
        wb_ref[...] = w_ref[...].astype(BF16)
        acc_s = _dot(hs_ref[...].astype(BF16), wb_ref[...])
        if has_res:
            acc_s = acc_s + rs_ref[...]
        os_ref[...] = acc_s.astype(os_ref.dtype)

    acc = _dot(hp_ref[...].astype(BF16), wb_ref[...])
    if has_res:
        acc = acc + rp_ref[...]
    op_ref[...] = acc.astype(op_ref.dtype)


def _proj(hp, hs, w, layer, *, res=None, out_dtype=F32, bm=1024, bn=512):
    mp, k = hp.shape
    ms = hs.shape[0]
    n = w.shape[-1]
    in_specs = [pl.BlockSpec((bm, k), lambda j, i: (i, 0)),
                pl.BlockSpec((ms, k), lambda j, i: (0, 0)),
                pl.BlockSpec((None, k, bn), lambda j, i: (layer, 0, j))]
    args = [hp, hs, w]
    if res is not None:
        in_specs += [pl.BlockSpec((bm, bn), lambda j, i: (i, j)),
                     pl.BlockSpec((ms, bn), lambda j, i: (0, j))]
        args += list(res)
    return pl.pallas_call(
        functools.partial(_proj_kernel, has_res=res is not None),
        grid=(n // bn, mp // bm),
        in_specs=in_specs,
        out_specs=[pl.BlockSpec((bm, bn), lambda j, i: (i, j)),
                   pl.BlockSpec((ms, bn), lambda j, i: (0, j))],
        out_shape=[jax.ShapeDtypeStruct((mp, n), out_dtype),
                   jax.ShapeDtypeStruct((ms, n), F32)],
        scratch_shapes=[pltpu.VMEM((k, bn), BF16)],
        compiler_params=_params("arbitrary", "arbitrary"),
        name="proj",
    )(*args)


def _matmul(h, w, layer, *, bm, bn):
    m, k = h.shape
    n = w.shape[-1]

    def body(h_ref, w_ref, o_ref, wb_ref):
        @pl.when(pl.program_id(1) == 0)
        def _():
            wb_ref[...] = w_ref[...].astype(BF16)
        o_ref[...] = _dot(h_ref[...], wb_ref[...])

    return pl.pallas_call(
        body,
        grid=(n // bn, m // bm),
        in_specs=[pl.BlockSpec((bm, k), lambda j, i: (i, 0)),
                  pl.BlockSpec((None, k, bn), lambda j, i: (layer, 0, j))],
        out_specs=pl.BlockSpec((bm, bn), lambda j, i: (i, j)),
        out_shape=jax.ShapeDtypeStruct((m, n), F32),
        scratch_shapes=[pltpu.VMEM((k, bn), BF16)],
        compiler_params=_params("arbitrary", "arbitrary"),
        name="matmul",
    )(h, w)


def _conv_chunks(rows):
    step = min(rows, 256)
    return [(s, step) for s in range(0, rows, step)]


def _conv3(ext_ref, w, start, size):
    out = w[0:1, :] * ext_ref[pl.ds(SUBLANE - 2 + start, size), :]
    out = out + w[1:2, :] * ext_ref[pl.ds(SUBLANE - 1 + start, size), :]
    return out + w[2:3, :] * ext_ref[pl.ds(SUBLANE + start, size), :]


def _carry_in(ext_ref, first_of_seq, bm):
    @pl.when(first_of_seq)
    def _():
        ext_ref[pl.ds(SUBLANE - 2, 2), :] = jnp.zeros((2, ext_ref.shape[1]), F32)

    @pl.when(jnp.logical_not(first_of_seq))
    def _():
        ext_ref[pl.ds(SUBLANE - 2, 2), :] = ext_ref[pl.ds(SUBLANE + bm - 2, 2), :]


def _silu(x):
    return x * (1.0 / (1.0 + jnp.exp(-x)))


def _ffn_up_kernel(hp_ref, hs_ref, wa_ref, wb_ref, cwa_ref, cwb_ref, psa_ref, psb_ref,
                   op_ref, os_ref, stpa_ref, stpb_ref, stsa_ref, stsb_ref,
                   w_scr, ea_ref, eb_ref, sa_ref, sb_ref, *, blocks_per_seq, n_s, t_new):
    i = pl.program_id(1)
    bm = hp_ref.shape[0]
    bn = wa_ref.shape[1]
    cwa = cwa_ref[...]
    cwb = cwb_ref[...]
    ext_s = t_new + SUBLANE

    @pl.when(i == 0)
    def _():
        w_scr[:, pl.ds(0, bn)] = wa_ref[...].astype(BF16)
        w_scr[:, pl.ds(bn, bn)] = wb_ref[...].astype(BF16)
        hs = hs_ref[...].astype(BF16)
        ua = _dot(hs, w_scr[:, pl.ds(0, bn)])
        ub = _dot(hs, w_scr[:, pl.ds(bn, bn)])
        for n in range(n_s):
            base = n * ext_s
            sa_ref[pl.ds(base + SUBLANE - 2, 2), :] = psa_ref[n]
            sb_ref[pl.ds(base + SUBLANE - 2, 2), :] = psb_ref[n]
            sa_ref[pl.ds(base + SUBLANE, t_new), :] = ua[n * t_new:(n + 1) * t_new]
            sb_ref[pl.ds(base + SUBLANE, t_new), :] = ub[n * t_new:(n + 1) * t_new]
        for n in range(n_s):
            base = n * ext_s
            ca = _conv3(sa_ref, cwa, base, t_new)
            cb = _conv3(sb_ref, cwb, base, t_new)
            os_ref[pl.ds(n * t_new, t_new), :] = (_silu(ca) * cb).astype(os_ref.dtype)
            stsa_ref[n] = sa_ref[pl.ds(base + SUBLANE + t_new - 2, 2), :]
            stsb_ref[n] = sb_ref[pl.ds(base + SUBLANE + t_new - 2, 2), :]

    first = (i % blocks_per_seq) == 0
    _carry_in(ea_ref, first, bm)
    _carry_in(eb_ref, first, bm)
    hp = hp_ref[...]
    ea_ref[pl.ds(SUBLANE, bm), :] = _dot(hp, w_scr[:, pl.ds(0, bn)])
    eb_ref[pl.ds(SUBLANE, bm), :] = _dot(hp, w_scr[:, pl.ds(bn, bn)])
    for s, n in _conv_chunks(bm):
        ca = _conv3(ea_ref, cwa, s, n)
        cb = _conv3(eb_ref, cwb, s, n)
        op_ref[pl.ds(s, n), :] = (_silu(ca) * cb).astype(op_ref.dtype)

    @pl.when((i % blocks_per_seq) == blocks_per_seq - 1)
    def _():
        stpa_ref[...] = ea_ref[pl.ds(SUBLANE + bm - 2, 2), :]
        stpb_ref[...] = eb_ref[pl.ds(SUBLANE + bm - 2, 2), :]


def _ffn_up(hp, hs, w, conv_w, prev_s, layer, n_p, seq, n_s, t_new, *, bm=1024, bn=512):
    mp, k = hp.shape
    ms = hs.shape[0]
    f = w.shape[-1] // 2
    nc = f // bn
    bps = seq // bm
    st = CONV_W - 1
    outs = pl.pallas_call(
        functools.partial(_ffn_up_kernel, blocks_per_seq=bps, n_s=n_s, t_new=t_new),
        grid=(nc, mp // bm),
        in_specs=[pl.BlockSpec((bm, k), lambda j, i: (i, 0)),
                  pl.BlockSpec((ms, k), lambda j, i: (0, 0)),
                  pl.BlockSpec((None, k, bn), lambda j, i: (layer, 0, j)),
                  pl.BlockSpec((None, k, bn), lambda j, i: (layer, 0, nc + j)),
                  pl.BlockSpec((None, CONV_W, bn), lambda j, i: (layer, 0, j)),
                  pl.BlockSpec((None, CONV_W, bn), lambda j, i: (layer, 0, nc + j)),
                  pl.BlockSpec((None, n_s, st, bn), lambda j, i: (layer, 0, 0, j)),
                  pl.BlockSpec((None, n_s, st, bn), lambda j, i: (layer, 0, 0, nc + j))],
        out_specs=[pl.BlockSpec((bm, bn), lambda j, i: (i, j)),
                   pl.BlockSpec((ms, bn), lambda j, i: (0, j)),
                   pl.BlockSpec((None, st, bn), lambda j, i: (i // bps, 0, j)),
                   pl.BlockSpec((None, st, bn), lambda j, i: (i // bps, 0, j)),
                   pl.BlockSpec((n_s, st, bn), lambda j, i: (0, 0, j)),
                   pl.BlockSpec((n_s, st, bn), lambda j, i: (0, 0, j))],
        out_shape=[jax.ShapeDtypeStruct((mp, f), BF16),
                   jax.ShapeDtypeStruct((ms, f), BF16),
                   jax.ShapeDtypeStruct((n_p, st, f), F32),
                   jax.ShapeDtypeStruct((n_p, st, f), F32),
                   jax.ShapeDtypeStruct((n_s, st, f), F32),
                   jax.ShapeDtypeStruct((n_s, st, f), F32)],
        scratch_shapes=[pltpu.VMEM((k, 2 * bn), BF16),
                        pltpu.VMEM((bm + SUBLANE, bn), F32),
                        pltpu.VMEM((bm + SUBLANE, bn), F32),
                        pltpu.VMEM((n_s * (t_new + SUBLANE), bn), F32),
                        pltpu.VMEM((n_s * (t_new + SUBLANE), bn), F32)],
        compiler_params=_params("arbitrary", "arbitrary"),
        name="ffn_up",
    )(hp, hs, w, w, conv_w, conv_w, prev_s, prev_s)
    gp, gs, stpa, stpb, stsa, stsb = outs
    return gp, gs, jnp.concatenate([stpa, stpb], axis=-1), jnp.concatenate([stsa, stsb], axis=-1)


def _mix_b_kernel(hp_ref, hs_ref, wb_ref, wc_ref, wv_ref, cw_ref, ps_ref,
                  op_ref, os_ref, stp_ref, sts_ref,
                  w_scr, e_ref, s_ref, *, blocks_per_seq, n_s, t_new):
    i = pl.program_id(1)
    bm = hp_ref.shape[0]
    bn = wb_ref.shape[1]
    cw = cw_ref[...]
    ext_s = t_new + SUBLANE

    @pl.when(i == 0)
    def _():
        w_scr[:, pl.ds(0, bn)] = wb_ref[...].astype(BF16)
        w_scr[:, pl.ds(bn, bn)] = wc_ref[...].astype(BF16)
        w_scr[:, pl.ds(2 * bn, bn)] = wv_ref[...].astype(BF16)
        hs = hs_ref[...].astype(BF16)
        gb = _dot(hs, w_scr[:, pl.ds(0, bn)])
        cv = _dot(hs, w_scr[:, pl.ds(bn, bn)]) * _dot(hs, w_scr[:, pl.ds(2 * bn, bn)])
        for n in range(n_s):
            base = n * ext_s
            s_ref[pl.ds(base + SUBLANE - 2, 2), :] = ps_ref[n]
            s_ref[pl.ds(base + SUBLANE, t_new), :] = cv[n * t_new:(n + 1) * t_new]
        for n in range(n_s):
            base = n * ext_s
            c = _conv3(s_ref, cw, base, t_new)
            os_ref[pl.ds(n * t_new, t_new), :] = (gb[n * t_new:(n + 1) * t_new] * c).astype(os_ref.dtype)
            sts_ref[n] = s_ref[pl.ds(base + SUBLANE + t_new - 2, 2), :]

    _carry_in(e_ref, (i % blocks_per_seq) == 0, bm)
    hp = hp_ref[...]
    e_ref[pl.ds(SUBLANE, bm), :] = _dot(hp, w_scr[:, pl.ds(bn, bn)]) * _dot(hp, w_scr[:, pl.ds(2 * bn, bn)])
    gb = _dot(hp, w_scr[:, pl.ds(0, bn)])
    for s, n in _conv_chunks(bm):
        op_ref[pl.ds(s, n), :] = (gb[s:s + n] * _conv3(e_ref, cw, s, n)).astype(op_ref.dtype)

    @pl.when((i % blocks_per_seq) == blocks_per_seq - 1)
    def _():
        stp_ref[...] = e_ref[pl.ds(SUBLANE + bm - 2, 2), :]


def _mix_b(hp, hs, w, conv_w, prev_s, layer, n_p, seq, n_s, t_new, *, bm=512, bn=512):
    mp, k = hp.shape
    ms = hs.shape[0]
    d = w.shape[-1] // 3
    nc = d // bn
    bps = seq // bm
    st = CONV_W - 1
    return pl.pallas_call(
        functools.partial(_mix_b_kernel, blocks_per_seq=bps, n_s=n_s, t_new=t_new),
        grid=(nc, mp // bm),
        in_specs=[pl.BlockSpec((bm, k), lambda j, i: (i, 0)),
                  pl.BlockSpec((ms, k), lambda j, i: (0, 0)),
                  pl.BlockSpec((None, k, bn), lambda j, i: (layer, 0, j)),
                  pl.BlockSpec((None, k, bn), lambda j, i: (layer, 0, nc + j)),
                  pl.BlockSpec((None, k, bn), lambda j, i: (layer, 0, 2 * nc + j)),
                  pl.BlockSpec((None, CONV_W, bn), lambda j, i: (layer, 0, j)),
                  pl.BlockSpec((None, n_s, st, bn), lambda j, i: (layer, 0, 0, j))],
        out_specs=[pl.BlockSpec((bm, bn), lambda j, i: (i, j)),
                   pl.BlockSpec((ms, bn), lambda j, i: (0, j)),
                   pl.BlockSpec((None, st, bn), lambda j, i: (i // bps, 0, j)),
                   pl.BlockSpec((n_s, st, bn), lambda j, i: (0, 0, j))],
        out_shape=[jax.ShapeDtypeStruct((mp, d), BF16),
                   jax.ShapeDtypeStruct((ms, d), BF16),
                   jax.ShapeDtypeStruct((n_p, st, d), F32),
                   jax.ShapeDtypeStruct((n_s, st, d), F32)],
        scratch_shapes=[pltpu.VMEM((k, 3 * bn), BF16),
                        pltpu.VMEM((bm + SUBLANE, bn), F32),
                        pltpu.VMEM((n_s * (t_new + SUBLANE), bn), F32)],
        compiler_params=_params("arbitrary", "arbitrary"),
        name="mix_b",
    )(hp, hs, w, w, w, conv_w, prev_s)


def _attn_block(q, k, v, valid):
    s = _dot_nt(q.astype(BF16), k.astype(BF16)) * (HD_A ** -0.5)
    s = jnp.where(valid, s, NEG)
    m = jnp.max(s, axis=-1, keepdims=True)
    p = jnp.exp(s - m)
    l = jnp.sum(p, axis=-1, keepdims=True)
    acc = _dot(p.astype(BF16), v.astype(BF16))
    return acc, m, l


def _attn_prompt_kernel(*refs, seq):
    qkv = refs[:9]
    o_ref, acc_ref, m_ref, l_ref = refs[9:]

    qi = lax.broadcasted_iota(jnp.int32, (BLOCK, 2 * BLOCK), 0) + BLOCK
    ki = lax.broadcasted_iota(jnp.int32, (BLOCK, 2 * BLOCK), 1)
    band_valid = (qi - ki >= 0) & (qi - ki <= BLOCK)
    ci = lax.broadcasted_iota(jnp.int32, (BLOCK, BLOCK), 0)
    cj = lax.broadcasted_iota(jnp.int32, (BLOCK, BLOCK), 1)
    first_valid = ci >= cj

    for g in range(N_GROUPS):
        q_ref, k_ref, v_ref = qkv[3 * g:3 * g + 3]
        dil = DILATIONS[g]
        n_blocks = seq // dil // BLOCK
        for r in range(dil):
            for b in range(n_blocks):
                q_rows = pl.ds(r + dil * b * BLOCK, BLOCK, stride=dil)
                if b == 0:
                    k_rows, valid = q_rows, first_valid
                else:
                    k_rows = pl.ds(r + dil * (b - 1) * BLOCK, 2 * BLOCK, stride=dil)
                    valid = band_valid
                acc, m, l = _attn_block(q_ref[q_rows, :], k_ref[k_rows, :], v_ref[k_rows, :], valid)
                acc_ref[g, q_rows, :] = acc
                m_ref[g, q_rows, :] = jnp.broadcast_to(m, (BLOCK, LANE))
                l_ref[g, q_rows, :] = jnp.broadcast_to(l, (BLOCK, LANE))

    def merge(c, carry):
        rows = pl.ds(pl.multiple_of(c * BLOCK, BLOCK), BLOCK)
        ms = [m_ref[g, rows, :] for g in range(N_GROUPS)]
        m = jnp.maximum(jnp.maximum(ms[0], ms[1]), ms[2])
        num = jnp.zeros((BLOCK, HD_A), F32)
        den = jnp.zeros((BLOCK, LANE), F32)
        for g in range(N_GROUPS):
            e = jnp.exp(ms[g] - m)
            num = num + e * acc_ref[g, rows, :]
            den = den + e * l_ref[g, rows, :]
        o_ref[rows, :] = (num / den).astype(o_ref.dtype)
        return carry

    lax.fori_loop(0, seq // BLOCK, merge, 0)


def _attn_prompt(qkv, n_seq, seq):
    def col(g, c):
        return lambda n, h: (n, (g * 3 + c) * H_G + h)

    in_specs = [pl.BlockSpec((seq, HD_A), col(g, c)) for g in range(N_GROUPS) for c in range(3)]
    return pl.pallas_call(
        functools.partial(_attn_prompt_kernel, seq=seq),
        grid=(n_seq, H_G),
        in_specs=in_specs,
        out_specs=pl.BlockSpec((seq, HD_A), lambda n, h: (n, h)),
        out_shape=jax.ShapeDtypeStruct((n_seq * seq, H_G * HD_A), BF16),
        scratch_shapes=[pltpu.VMEM((N_GROUPS, seq, HD_A), F32),
                        pltpu.VMEM((N_GROUPS, seq, LANE), F32),
                        pltpu.VMEM((N_GROUPS, seq, LANE), F32)],
        compiler_params=_params("parallel", "parallel"),
        name="attn_prompt",
    )(*([qkv] * 9))


def _win_kernel(*refs):
    ins, outs = refs[:2 * N_GROUPS], refs[2 * N_GROUPS:]
    for g in range(N_GROUPS):
        keep = ins[2 * g].shape[0]
        for c in range(2):
            for h in range(H_G):
                outs[g][pl.ds(c * H_G + h, keep, stride=KV_ROWS), :] = ins[2 * g + c][:, h * HD_A:(h + 1) * HD_A]


def _prompt_windows(qkv, n_seq, seq):
    hw = H_G * HD_A
    in_specs, out_specs, out_shape = [], [], []
    for g in range(N_GROUPS):
        keep = min(WINDOWS[g], seq)
        per = seq // keep
        for c in (1, 2):
            in_specs.append(pl.BlockSpec((keep, hw), lambda n, g=g, c=c, per=per: (n * per + per - 1, g * 3 + c)))
        out_specs.append(pl.BlockSpec((None, keep * KV_ROWS, HD_A), lambda n: (n, 0, 0)))
        out_shape.append(jax.ShapeDtypeStruct((n_seq, keep * KV_ROWS, HD_A), F32))
    return pl.pallas_call(
        _win_kernel,
        grid=(n_seq,),
        in_specs=in_specs,
        out_specs=out_specs,
        out_shape=out_shape,
        compiler_params=_params("parallel"),
        name="prompt_windows",
    )(*([qkv] * (2 * N_GROUPS)))


def _attn_sample_kernel(qkv_ref, c0_ref, c1_ref, c2_ref, o_ref, w0_ref, w1_ref, w2_ref, *, t_new):
    caches = (c0_ref, c1_ref, c2_ref)
    wins = (w0_ref, w1_ref, w2_ref)
    hw = H_G * HD_A
    accs = [[] for _ in range(N_GROUPS)]
    ms = [[] for _ in range(N_GROUPS)]
    ls = [[] for _ in range(N_GROUPS)]
    for g in range(N_GROUPS):
        c_ref, w_ref = caches[g], wins[g]
        n_buf = c_ref.shape[0] // KV_ROWS
        dil = DILATIONS[g]
        n_back = WINDOWS[g] // dil
        base = g * 3 * hw
        keep_rows = (n_buf - t_new) * KV_ROWS
        w_ref[pl.ds(0, keep_rows), :] = c_ref[pl.ds(t_new * KV_ROWS, keep_rows), :]
        for c in range(2):
            for h in range(H_G):
                col = base + (1 + c) * hw + h * HD_A
                w_ref[pl.ds(keep_rows + c * H_G + h, t_new, stride=KV_ROWS), :] = qkv_ref[:, col:col + HD_A]

        t_w = lax.broadcasted_iota(jnp.int32, (t_new, n_buf), 0)
        c_w = lax.broadcasted_iota(jnp.int32, (t_new, n_buf), 1)
        diff_w = n_buf + t_w - (c_w + t_new)
        valid_w = (diff_w >= 0) & ((diff_w & (dil - 1)) == 0) & (diff_w <= dil * n_back)
        t_o = lax.broadcasted_iota(jnp.int32, (t_new, LANE), 0)
        c_o = lax.broadcasted_iota(jnp.int32, (t_new, LANE), 1)
        diff_o = n_buf + t_o - c_o
        valid_o = (c_o < t_new) & ((diff_o & (dil - 1)) == 0) & (diff_o <= dil * n_back)

        for h in range(H_G):
            q = qkv_ref[:, base + h * HD_A: base + (h + 1) * HD_A].astype(BF16)
            k_w = w_ref[pl.ds(h, n_buf, stride=KV_ROWS), :].astype(BF16)
            v_w = w_ref[pl.ds(H_G + h, n_buf, stride=KV_ROWS), :].astype(BF16)
            k_o = c_ref[pl.ds(h, LANE, stride=KV_ROWS), :].astype(BF16)
            v_o = c_ref[pl.ds(H_G + h, LANE, stride=KV_ROWS), :].astype(BF16)
            s_w = jnp.where(valid_w, _dot_nt(q, k_w) * (HD_A ** -0.5), NEG)
            s_o = jnp.where(valid_o, _dot_nt(q, k_o) * (HD_A ** -0.5), NEG)
            m = jnp.maximum(jnp.max(s_w, axis=-1, keepdims=True), jnp.max(s_o, axis=-1, keepdims=True))
            p_w = jnp.exp(s_w - m)
            p_o = jnp.exp(s_o - m)
            l = jnp.sum(p_w, axis=-1, keepdims=True) + jnp.sum(p_o, axis=-1, keepdims=True)
            acc = _dot(p_w.astype(BF16), v_w) + _dot(p_o.astype(BF16), v_o)
            accs[g].append(acc)
            ms[g].append(m)
            ls[g].append(l)

    for h in range(H_G):
        m = jnp.maximum(jnp.maximum(ms[0][h], ms[1][h]), ms[2][h])
        num = jnp.zeros((t_new, HD_A), F32)
        den = jnp.zeros((t_new, 1), F32)
        for g in range(N_GROUPS):
            e = jnp.exp(ms[g][h] - m)
            num = num + e * accs[g][h]
            den = den + e * ls[g][h]
        o_ref[:, h * HD_A:(h + 1) * HD_A] = num / den


def _attn_sample(qkv, caches, layer, n_seq, t_new):
    in_specs = [pl.BlockSpec((t_new, qkv.shape[1]), lambda n: (n, 0))]
    out_specs = [pl.BlockSpec((t_new, H_G * HD_A), lambda n: (n, 0))]
    out_shape = [jax.ShapeDtypeStruct((n_seq * t_new, H_G * HD_A), F32)]
    for c in caches:
        rows = c.shape[2]
        in_specs.append(pl.BlockSpec((None, None, rows, HD_A), lambda n: (layer, n, 0, 0)))
        out_specs.append(pl.BlockSpec((None, rows, HD_A), lambda n: (n, 0, 0)))
        out_shape.append(jax.ShapeDtypeStruct((n_seq, rows, HD_A), F32))
    return pl.pallas_call(
        functools.partial(_attn_sample_kernel, t_new=t_new),
        grid=(n_seq,),
        in_specs=in_specs,
        out_specs=out_specs,
        out_shape=out_shape,
        compiler_params=_params("parallel"),
        name="attn_sample",
    )(qkv, *caches)


def _softmax_pv(q, kb, vb, scale):
    s = _dot_nt(q, kb) * scale
    m = jnp.max(s, axis=-1, keepdims=True)
    p = jnp.exp(s - m)
    l = jnp.sum(p, axis=-1, keepdims=True)
    return _dot(p.astype(BF16), vb) / l


def _mem_attn_prompt_kernel(q_ref, k_ref, v_ref, o_ref, kv_ref):
    rows, hd = q_ref.shape
    h = pl.program_id(1)
    k = k_ref[...]
    v = v_ref[...]
    for hh in range(H_M):
        @pl.when(h == hh)
        def _(hh=hh):
            kv_ref[:, 0, hh, :] = k
            kv_ref[:, 1, hh, :] = v
    kb = k.astype(BF16)
    vb = v.astype(BF16)
    chunk = min(rows, 256)

    def body(c, carry):
        r = pl.ds(pl.multiple_of(c * chunk, chunk), chunk)
        o_ref[r, :] = _softmax_pv(q_ref[r, :], kb, vb, hd ** -0.5).astype(o_ref.dtype)
        return carry

    lax.fori_loop(0, rows // chunk, body, 0, unroll=True)


def _mem_attn_prompt(q, kv, n_seq, rows):
    hd = q.shape[1] // H_M
    n_mem = kv.shape[0] // n_seq
    return pl.pallas_call(
        _mem_attn_prompt_kernel,
        grid=(n_seq, H_M),
        in_specs=[pl.BlockSpec((rows, hd), lambda n, h: (n, h)),
                  pl.BlockSpec((n_mem, hd), lambda n, h: (n, h)),
                  pl.BlockSpec((n_mem, hd), lambda n, h: (n, H_M + h))],
        out_specs=[pl.BlockSpec((rows, hd), lambda n, h: (n, h)),
                   pl.BlockSpec((None, n_mem, 2, H_M, hd), lambda n, h: (n, 0, 0, 0, 0))],
        out_shape=[jax.ShapeDtypeStruct(q.shape, BF16),
                   jax.ShapeDtypeStruct((n_seq, n_mem, 2, H_M, hd), F32)],
        compiler_params=_params("parallel", "arbitrary"),
        name="mem_attn_prompt",
    )(q, kv, kv)


def _mem_attn_sample_kernel(q_ref, kv_ref, o_ref):
    hd = kv_ref.shape[-1]
    for h in range(H_M):
        kb = kv_ref[:, 0, h, :].astype(BF16)
        vb = kv_ref[:, 1, h, :].astype(BF16)
        q = q_ref[:, h * hd:(h + 1) * hd].astype(BF16)
        o_ref[:, h * hd:(h + 1) * hd] = _softmax_pv(q, kb, vb, hd ** -0.5)


def _mem_attn_sample(q, cache, layer, n_seq, rows):
    n_mem, _, _, hd = cache.shape[2:]
    return pl.pallas_call(
        _mem_attn_sample_kernel,
        grid=(n_seq,),
        in_specs=[pl.BlockSpec((rows, q.shape[1]), lambda n: (n, 0)),
                  pl.BlockSpec((None, None, n_mem, 2, H_M, hd), lambda n: (layer, n, 0, 0, 0, 0))],
        out_specs=pl.BlockSpec((rows, q.shape[1]), lambda n: (n, 0)),
        out_shape=jax.ShapeDtypeStruct(q.shape, F32),
        compiler_params=_params("parallel"),
        name="mem_attn_sample",
    )(q, cache)


def kernel(x_prompt, x_sample, mem_prompt, cache_win0_kv, cache_win1_kv, cache_win2_kv, state_conv_b, state_ffn_conv, cache_mem_kv, g_mix, g_mem_q, g_mem_kv, g_ffn, g_final, w_in_a, w_out_a, w_in_b, conv_b, w_out_b, w_q_mem, w_kv_mem, w_o_mem, w_up, conv_ffn, w_down):
    n_p, seq, d = x_prompt.shape
    n_s, t_new, _ = x_sample.shape
    depth = g_mix.shape[0]
    n_mem = mem_prompt.shape[1]

    yp = x_prompt.reshape(n_p * seq, d)
    ys = x_sample.reshape(n_s * t_new, d)
    mem = mem_prompt.reshape(n_p * n_mem, d)
    caches = [c.reshape(c.shape[0], c.shape[1], c.shape[2] * KV_ROWS, HD_A)
              for c in (cache_win0_kv, cache_win1_kv, cache_win2_kv)]

    win_p = [[] for _ in range(N_GROUPS)]
    win_s = [[] for _ in range(N_GROUPS)]
    conv_p, conv_s, ffn_p, ffn_s, mem_p = [], [], [], [], []

    for i in range(depth):
        j = i // 2
        hp = _rmsnorm(yp, g_mix[i], BF16)
        hs = _rmsnorm(ys, g_mix[i], BF16)
        if i % 2 == 0:
            qkv_p, qkv_s = _proj(hp, hs, w_in_a, j, bm=512, bn=3 * H_G * HD_A)
            o_p = _attn_prompt(qkv_p, n_p, seq)
            o_s, *new_wins = _attn_sample(qkv_s, caches, j, n_s, t_new)
            yp, ys = _proj(o_p, o_s, w_out_a, j, res=(yp, ys))
            wins = _prompt_windows(qkv_p, n_p, seq)
            for g in range(N_GROUPS):
                win_p[g].append(wins[g].reshape(n_p, -1, 2, H_G, HD_A))
                win_s[g].append(new_wins[g].reshape(n_s, -1, 2, H_G, HD_A))
        else:
            o_p, o_s, st_p, st_s = _mix_b(hp, hs, w_in_b, conv_b, state_conv_b, j, n_p, seq, n_s, t_new)
            yp, ys = _proj(o_p, o_s, w_out_b, j, res=(yp, ys))
            conv_p.append(st_p)
            conv_s.append(st_s)

        kv_p = _matmul(_rmsnorm(mem, g_mem_kv[i], BF16), w_kv_mem, i, bm=n_p * n_mem, bn=512)
        q_p, q_s = _proj(_rmsnorm(yp, g_mem_q[i], BF16), _rmsnorm(ys, g_mem_q[i], BF16), w_q_mem, i,
                         out_dtype=BF16)
        a_p, kv_out = _mem_attn_prompt(q_p, kv_p, n_p, seq)
        mem_p.append(kv_out)
        a_s = _mem_attn_sample(q_s, cache_mem_kv, i, n_s, t_new)
        yp, ys = _proj(a_p, a_s, w_o_mem, i, res=(yp, ys))

        f_p, f_s, sf_p, sf_s = _ffn_up(_rmsnorm(yp, g_ffn[i], BF16), _rmsnorm(ys, g_ffn[i], BF16),
                                       w_up, conv_ffn, state_ffn_conv, i, n_p, seq, n_s, t_new)
        ffn_p.append(sf_p)
        ffn_s.append(sf_s)
        yp, ys = _proj(f_p, f_s, w_down, i, res=(yp, ys), bm=512)

    y_prompt = _rmsnorm(yp, g_final, F32).reshape(n_p, seq, d)
    y_sample = _rmsnorm(ys, g_final, F32).reshape(n_s, t_new, d)
    return (y_prompt, y_sample,
            jnp.stack(win_p[0]), jnp.stack(win_p[1]), jnp.stack(win_p[2]),
            jnp.stack(conv_p), jnp.stack(ffn_p), jnp.stack(mem_p),
            jnp.stack(win_s[0]), jnp.stack(win_s[1]), jnp.stack(win_s[2]),
            jnp.stack(conv_s), jnp.stack(ffn_s))
```

```python
import functools

import jax
import jax.numpy as jnp
from jax import lax
from jax.experimental import pallas as pl
from jax.experimental.pallas import tpu as pltpu

F32 = jnp.float32
BF16 = jnp.bfloat16

WINDOWS = (128, 512, 2048)
DILATIONS = (1, 4, 16)
N_GROUPS = 3
H_G = 4
HD_A = 128
BLOCK = 128
CONV_W = 3
H_M = 4
EPS = 1e-6
NEG = -1e30

LANE = 128
SUBLANE = 8
VMEM_LIMIT = 56 * 1024 * 1024
KV_ROWS = 2 * H_G


def _params(*sem):
    return pltpu.CompilerParams(dimension_semantics=sem, vmem_limit_bytes=VMEM_LIMIT)


def _dot(a, b):
    return jnp.dot(a, b, preferred_element_type=F32)


def _dot_nt(a, b):
    return lax.dot_general(a, b, (((1,), (1,)), ((), ())), preferred_element_type=F32)


def _norm_kernel(x_ref, g_ref, o_ref):
    x = x_ref[...]
    ms = jnp.mean(x * x, axis=-1, keepdims=True)
    o_ref[...] = (x * lax.rsqrt(ms + EPS) * g_ref[...]).astype(o_ref.dtype)


def _rmsnorm(x, g, out_dtype):
    m, d = x.shape
    bm = min(m, 512)
    return pl.pallas_call(
        _norm_kernel,
        grid=(m // bm,),
        in_specs=[pl.BlockSpec((bm, d), lambda i: (i, 0)),
                  pl.BlockSpec((1, d), lambda i: (0, 0))],
        out_specs=pl.BlockSpec((bm, d), lambda i: (i, 0)),
        out_shape=jax.ShapeDtypeStruct((m, d), out_dtype),
        compiler_params=_params("parallel"),
        name="rmsnorm",
    )(x, g.reshape(1, d))


def _proj_kernel(*refs, has_res):
    if has_res:
        hp_ref, hs_ref, w_ref, rp_ref, rs_ref, op_ref, os_ref, wb_ref = refs
    else:
        hp_ref, hs_ref, w_ref, op_ref, os_ref, wb_ref = refs

    @pl.when(pl.program_id(1) == 0)
    def _():
        wb_ref[...] = w_ref[...].astype(BF16)
        acc_s = _dot(hs_ref[...].astype(BF16), wb_ref[...])
        if has_res:
            acc_s = acc_s + rs_ref[...]
        os_ref[...] = acc_s.astype(os_ref.dtype)

    acc = _dot(hp_ref[...].astype(BF16), wb_ref[...])
    if has_res:
        acc = acc + rp_ref[...]
    op_ref[...] = acc.astype(op_ref.dtype)


def _proj(hp, hs, w, layer, *, res=None, out_dtype=F32, bm=1024, bn=512):
    mp, k = hp.shape
    ms = hs.shape[0]
    n = w.shape[-1]
    in_specs = [pl.BlockSpec((bm, k), lambda j, i: (i, 0)),
                pl.BlockSpec((ms, k), lambda j, i: (0, 0)),
                pl.BlockSpec((None, k, bn), lambda j, i: (layer, 0, j))]
    args = [hp, hs, w]
    if res is not None:
        in_specs += [pl.BlockSpec((bm, bn), lambda j, i: (i, j)),
                     pl.BlockSpec((ms, bn), lambda j, i: (0, j))]
        args += list(res)
    return pl.pallas_call(
        functools.partial(_proj_kernel, has_res=res is not None),
        grid=(n // bn, mp // bm),
        in_specs=in_specs,
        out_specs=[pl.BlockSpec((bm, bn), lambda j, i: (i, j)),
                   pl.BlockSpec((ms, bn), lambda j, i: (0, j))],
        out_shape=[jax.ShapeDtypeStruct((mp, n), out_dtype),
                   jax.ShapeDtypeStruct((ms, n), F32)],
        scratch_shapes=[pltpu.VMEM((k, bn), BF16)],
        compiler_params=_params("arbitrary", "arbitrary"),
        name="proj",
    )(*args)


def _matmul(h, w, layer, *, bm, bn):
    m, k = h.shape
    n = w.shape[-1]

    def body(h_ref, w_ref, o_ref, wb_ref):
        @pl.when(pl.program_id(1) == 0)
        def _():
            wb_ref[...] = w_ref[...].astype(BF16)
        o_ref[...] = _dot(h_ref[...], wb_ref[...])

    return pl.pallas_call(
        body,
        grid=(n // bn, m // bm),
        in_specs=[pl.BlockSpec((bm, k), lambda j, i: (i, 0)),
                  pl.BlockSpec((None, k, bn), lambda j, i: (layer, 0, j))],
        out_specs=pl.BlockSpec((bm, bn), lambda j, i: (i, j)),
        out_shape=jax.ShapeDtypeStruct((m, n), F32),
        scratch_shapes=[pltpu.VMEM((k, bn), BF16)],
        compiler_params=_params("arbitrary", "arbitrary"),
        name="matmul",
    )(h, w)


def _conv_chunks(rows):
    step = min(rows, 256)
    return [(s, step) for s in range(0, rows, step)]


def _conv3(ext_ref, w, start, size):
    out = w[0:1, :] * ext_ref[pl.ds(SUBLANE - 2 + start, size), :]
    out = out + w[1:2, :] * ext_ref[pl.ds(SUBLANE - 1 + start, size), :]
    return out + w[2:3, :] * ext_ref[pl.ds(SUBLANE + start, size), :]


def _carry_in(ext_ref, first_of_seq, bm):
    @pl.when(first_of_seq)
    def _():
        ext_ref[pl.ds(SUBLANE - 2, 2), :] = jnp.zeros((2, ext_ref.shape[1]), F32)

    @pl.when(jnp.logical_not(first_of_seq))
    def _():
        ext_ref[pl.ds(SUBLANE - 2, 2), :] = ext_ref[pl.ds(SUBLANE + bm - 2, 2), :]


def _silu(x):
    return x * (1.0 / (1.0 + jnp.exp(-x)))


def _ffn_up_kernel(hp_ref, hs_ref, wa_ref, wb_ref, cwa_ref, cwb_ref, psa_ref, psb_ref,
                   op_ref, os_ref, stpa_ref, stpb_ref, stsa_ref, stsb_ref,
                   w_scr, ea_ref, eb_ref, sa_ref, sb_ref, *, blocks_per_seq, n_s, t_new):
    i = pl.program_id(1)
    bm = hp_ref.shape[0]
    bn = wa_ref.shape[1]
    cwa = cwa_ref[...]
    cwb = cwb_ref[...]
    ext_s = t_new + SUBLANE

    @pl.when(i == 0)
    def _():
        w_scr[:, pl.ds(0, bn)] = wa_ref[...].astype(BF16)
        w_scr[:, pl.ds(bn, bn)] = wb_ref[...].astype(BF16)
        hs = hs_ref[...].astype(BF16)
        ua = _dot(hs, w_scr[:, pl.ds(0, bn)])
        ub = _dot(hs, w_scr[:, pl.ds(bn, bn)])
        for n in range(n_s):
            base = n * ext_s
            sa_ref[pl.ds(base + SUBLANE - 2, 2), :] = psa_ref[n]
            sb_ref[pl.ds(base + SUBLANE - 2, 2), :] = psb_ref[n]
            sa_ref[pl.ds(base + SUBLANE, t_new), :] = ua[n * t_new:(n + 1) * t_new]
            sb_ref[pl.ds(base + SUBLANE, t_new), :] = ub[n * t_new:(n + 1) * t_new]
        for n in range(n_s):
            base = n * ext_s
            ca = _conv3(sa_ref, cwa, base, t_new)
            cb = _conv3(sb_ref, cwb, base, t_new)
            os_ref[pl.ds(n * t_new, t_new), :] = (_silu(ca) * cb).astype(os_ref.dtype)
            stsa_ref[n] = sa_ref[pl.ds(base + SUBLANE + t_new - 2, 2), :]
            stsb_ref[n] = sb_ref[pl.ds(base + SUBLANE + t_new - 2, 2), :]

    first = (i % blocks_per_seq) == 0
    _carry_in(ea_ref, first, bm)
    _carry_in(eb_ref, first, bm)
    hp = hp_ref[...]
    ea_ref[pl.ds(SUBLANE, bm), :] = _dot(hp, w_scr[:, pl.ds(0, bn)])
    eb_ref[pl.ds(SUBLANE, bm), :] = _dot(hp, w_scr[:, pl.ds(bn, bn)])
    for s, n in _conv_chunks(bm):
        ca = _conv3(ea_ref, cwa, s, n)
        cb = _conv3(eb_ref, cwb, s, n)
        op_ref[pl.ds(s, n), :] = (_silu(ca) * cb).astype(op_ref.dtype)

    @pl.when((i % blocks_per_seq) == blocks_per_seq - 1)
    def _():
        stpa_ref[...] = ea_ref[pl.ds(SUBLANE + bm - 2, 2), :]
        stpb_ref[...] = eb_ref[pl.ds(SUBLANE + bm - 2, 2), :]


def _ffn_up(hp, hs, w, conv_w, prev_s, layer, n_p, seq, n_s, t_new, *, bm=1024, bn=512):
    mp, k = hp.shape
    ms = hs.shape[0]
    f = w.shape[-1] // 2
    nc = f // bn
    bps = seq // bm
    st = CONV_W - 1
    outs = pl.pallas_call(
        functools.partial(_ffn_up_kernel, blocks_per_seq=bps, n_s=n_s, t_new=t_new),
        grid=(nc, mp // bm),
        in_specs=[pl.BlockSpec((bm, k), lambda j, i: (i, 0)),
                  pl.BlockSpec((ms, k), lambda j, i: (0, 0)),
                  pl.BlockSpec((None, k, bn), lambda j, i: (layer, 0, j)),
                  pl.BlockSpec((None, k, bn), lambda j, i: (layer, 0, nc + j)),
                  pl.BlockSpec((None, CONV_W, bn), lambda j, i: (layer, 0, j)),
                  pl.BlockSpec((None, CONV_W, bn), lambda j, i: (layer, 0, nc + j)),
                  pl.BlockSpec((None, n_s, st, bn), lambda j, i: (layer, 0, 0, j)),
                  pl.BlockSpec((None, n_s, st, bn), lambda j, i: (layer, 0, 0, nc + j))],
        out_specs=[pl.BlockSpec((bm, bn), lambda j, i: (i, j)),
                   pl.BlockSpec((ms, bn), lambda j, i: (0, j)),
                   pl.BlockSpec((None, st, bn), lambda j, i: (i // bps, 0, j)),
                   pl.BlockSpec((None, st, bn), lambda j, i: (i // bps, 0, j)),
                   pl.BlockSpec((n_s, st, bn), lambda j, i: (0, 0, j)),
                   pl.BlockSpec((n_s, st, bn), lambda j, i: (0, 0, j))],
        out_shape=[jax.ShapeDtypeStruct((mp, f), BF16),
                   jax.ShapeDtypeStruct((ms, f), BF16),
                   jax.ShapeDtypeStruct((n_p, st, f), F32),
                   jax.ShapeDtypeStruct((n_p, st, f), F32),
                   jax.ShapeDtypeStruct((n_s, st, f), F32),
                   jax.ShapeDtypeStruct((n_s, st, f), F32)],
        scratch_shapes=[pltpu.VMEM((k, 2 * bn), BF16),
                        pltpu.VMEM((bm + SUBLANE, bn), F32),
                        pltpu.VMEM((bm + SUBLANE, bn), F32),
                        pltpu.VMEM((n_s * (t_new + SUBLANE), bn), F32),
                        pltpu.VMEM((n_s * (t_new + SUBLANE), bn), F32)],
        compiler_params=_params("arbitrary", "arbitrary"),
        name="ffn_up",
    )(hp, hs, w, w, conv_w, conv_w, prev_s, prev_s)
    gp, gs, stpa, stpb, stsa, stsb = outs
    return gp, gs, jnp.concatenate([stpa, stpb], axis=-1), jnp.concatenate([stsa, stsb], axis=-1)


def _mix_b_kernel(hp_ref, hs_ref, wb_ref, wc_ref, wv_ref, cw_ref, ps_ref,
                  op_ref, os_ref, stp_ref, sts_ref,
                  w_scr, e_ref, s_ref, *, blocks_per_seq, n_s, t_new):
    i = pl.program_id(1)
    bm = hp_ref.shape[0]
    bn = wb_ref.shape[1]
    cw = cw_ref[...]
    ext_s = t_new + SUBLANE

    @pl.when(i == 0)
    def _():
        w_scr[:, pl.ds(0, bn)] = wb_ref[...].astype(BF16)
        w_scr[:, pl.ds(bn, bn)] = wc_ref[...].astype(BF16)
        w_scr[:, pl.ds(2 * bn, bn)] = wv_ref[...].astype(BF16)
        hs = hs_ref[...].astype(BF16)
        gb = _dot(hs, w_scr[:, pl.ds(0, bn)])
        cv = _dot(hs, w_scr[:, pl.ds(bn, bn)]) * _dot(hs, w_scr[:, pl.ds(2 * bn, bn)])
        for n in range(n_s):
            base = n * ext_s
            s_ref[pl.ds(base + SUBLANE - 2, 2), :] = ps_ref[n]
            s_ref[pl.ds(base + SUBLANE, t_new), :] = cv[n * t_new:(n + 1) * t_new]
        for n in range(n_s):
            base = n * ext_s
            c = _conv3(s_ref, cw, base, t_new)
            os_ref[pl.ds(n * t_new, t_new), :] = (gb[n * t_new:(n + 1) * t_new] * c).astype(os_ref.dtype)
            sts_ref[n] = s_ref[pl.ds(base + SUBLANE + t_new - 2, 2), :]

    _carry_in(e_ref, (i % blocks_per_seq) == 0, bm)
    hp = hp_ref[...]
    e_ref[pl.ds(SUBLANE, bm), :] = _dot(hp, w_scr[:, pl.ds(bn, bn)]) * _dot(hp, w_scr[:, pl.ds(2 * bn, bn)])
    gb = _dot(hp, w_scr[:, pl.ds(0, bn)])
    for s, n in _conv_chunks(bm):
        op_ref[pl.ds(s, n), :] = (gb[s:s + n] * _conv3(e_ref, cw, s, n)).astype(op_ref.dtype)

    @pl.when((i % blocks_per_seq) == blocks_per_seq - 1)
    def _():
        stp_ref[...] = e_ref[pl.ds(SUBLANE + bm - 2, 2), :]


def _mix_b(hp, hs, w, conv_w, prev_s, layer, n_p, seq, n_s, t_new, *, bm=512, bn=512):
    mp, k = hp.shape
    ms = hs.shape[0]
    d = w.shape[-1] // 3
    nc = d // bn
    bps = seq // bm
    st = CONV_W - 1
    return pl.pallas_call(
        functools.partial(_mix_b_kernel, blocks_per_seq=bps, n_s=n_s, t_new=t_new),
        grid=(nc, mp // bm),
        in_specs=[pl.BlockSpec((bm, k), lambda j, i: (i, 0)),
                  pl.BlockSpec((ms, k), lambda j, i: (0, 0)),
                  pl.BlockSpec((None, k, bn), lambda j, i: (layer, 0, j)),
                  pl.BlockSpec((None, k, bn), lambda j, i: (layer, 0, nc + j)),
                  pl.BlockSpec((None, k, bn), lambda j, i: (layer, 0, 2 * nc + j)),
                  pl.BlockSpec((None, CONV_W, bn), lambda j, i: (layer, 0, j)),
                  pl.BlockSpec((None, n_s, st, bn), lambda j, i: (layer, 0, 0, j))],
        out_specs=[pl.BlockSpec((bm, bn), lambda j, i: (i, j)),
                   pl.BlockSpec((ms, bn), lambda j, i: (0, j)),
                   pl.BlockSpec((None, st, bn), lambda j, i: (i // bps, 0, j)),
                   pl.BlockSpec((n_s, st, bn), lambda j, i: (0, 0, j))],
        out_shape=[jax.ShapeDtypeStruct((mp, d), BF16),
                   jax.ShapeDtypeStruct((ms, d), BF16),
                   jax.ShapeDtypeStruct((n_p, st, d), F32),
                   jax.ShapeDtypeStruct((n_s, st, d), F32)],
        scratch_shapes=[pltpu.VMEM((k, 3 * bn), BF16),
                        pltpu.VMEM((bm + SUBLANE, bn), F32),
                        pltpu.VMEM((n_s * (t_new + SUBLANE), bn), F32)],
        compiler_params=_params("arbitrary", "arbitrary"),
        name="mix_b",
    )(hp, hs, w, w, w, conv_w, prev_s)


def _attn_block(q, k, v, valid):
    s = _dot_nt(q.astype(BF16), k.astype(BF16)) * (HD_A ** -0.5)
    s = jnp.where(valid, s, NEG)
    m = jnp.max(s, axis=-1, keepdims=True)
    p = jnp.exp(s - m)
    l = jnp.sum(p, axis=-1, keepdims=True)
    acc = _dot(p.astype(BF16), v.astype(BF16))
    return acc, m, l


def _attn_prompt_kernel(*refs, seq):
    qkv = refs[:9]
    o_ref, acc_ref, m_ref, l_ref = refs[9:]

    qi = lax.broadcasted_iota(jnp.int32, (BLOCK, 2 * BLOCK), 0) + BLOCK
    ki = lax.broadcasted_iota(jnp.int32, (BLOCK, 2 * BLOCK), 1)
    band_valid = (qi - ki >= 0) & (qi - ki <= BLOCK)
    ci = lax.broadcasted_iota(jnp.int32, (BLOCK, BLOCK), 0)
    cj = lax.broadcasted_iota(jnp.int32, (BLOCK, BLOCK), 1)
    first_valid = ci >= cj

    for g in range(N_GROUPS):
        q_ref, k_ref, v_ref = qkv[3 * g:3 * g + 3]
        dil = DILATIONS[g]
        n_blocks = seq // dil // BLOCK
        for r in range(dil):
            for b in range(n_blocks):
                q_rows = pl.ds(r + dil * b * BLOCK, BLOCK, stride=dil)
                if b == 0:
                    k_rows, valid = q_rows, first_valid
                else:
                    k_rows = pl.ds(r + dil * (b - 1) * BLOCK, 2 * BLOCK, stride=dil)
                    valid = band_valid
                acc, m, l = _attn_block(q_ref[q_rows, :], k_ref[k_rows, :], v_ref[k_rows, :], valid)
                acc_ref[g, q_rows, :] = acc
                m_ref[g, q_rows, :] = jnp.broadcast_to(m, (BLOCK, LANE))
                l_ref[g, q_rows, :] = jnp.broadcast_to(l, (BLOCK, LANE))

    def merge(c, carry):
        rows = pl.ds(pl.multiple_of(c * BLOCK, BLOCK), BLOCK)
        ms = [m_ref[g, rows, :] for g in range(N_GROUPS)]
        m = jnp.maximum(jnp.maximum(ms[0], ms[1]), ms[2])
        num = jnp.zeros((BLOCK, HD_A), F32)
        den = jnp.zeros((BLOCK, LANE), F32)
        for g in range(N_GROUPS):
            e = jnp.exp(ms[g] - m)
            num = num + e * acc_ref[g, rows, :]
            den = den + e * l_ref[g, rows, :]
        o_ref[rows, :] = (num / den).astype(o_ref.dtype)
        return carry

    lax.fori_loop(0, seq // BLOCK, merge, 0, unroll=4)


def _attn_prompt(qkv, n_seq, seq):
    def col(g, c):
        return lambda n, h: (n, (g * 3 + c) * H_G + h)

    in_specs = [pl.BlockSpec((seq, HD_A), col(g, c)) for g in range(N_GROUPS) for c in range(3)]
    return pl.pallas_call(
        functools.partial(_attn_prompt_kernel, seq=seq),
        grid=(n_seq, H_G),
        in_specs=in_specs,
        out_specs=pl.BlockSpec((seq, HD_A), lambda n, h: (n, h)),
        out_shape=jax.ShapeDtypeStruct((n_seq * seq, H_G * HD_A), BF16),
        scratch_shapes=[pltpu.VMEM((N_GROUPS, seq, HD_A), F32),
                        pltpu.VMEM((N_GROUPS, seq, LANE), F32),
                        pltpu.VMEM((N_GROUPS, seq, LANE), F32)],
        compiler_params=_params("parallel", "parallel"),
        name="attn_prompt",
    )(*([qkv] * 9))


def _win_kernel(*refs):
    ins, outs = refs[:2 * N_GROUPS], refs[2 * N_GROUPS:]
    for g in range(N_GROUPS):
        keep = ins[2 * g].shape[0]
        for c in range(2):
            for h in range(H_G):
                outs[g][pl.ds(c * H_G + h, keep, stride=KV_ROWS), :] = ins[2 * g + c][:, h * HD_A:(h + 1) * HD_A]


def _prompt_windows(qkv, n_seq, seq):
    hw = H_G * HD_A
    in_specs, out_specs, out_shape = [], [], []
    for g in range(N_GROUPS):
        keep = min(WINDOWS[g], seq)
        per = seq // keep
        for c in (1, 2):
            in_specs.append(pl.BlockSpec((keep, hw), lambda n, g=g, c=c, per=per: (n * per + per - 1, g * 3 + c)))
        out_specs.append(pl.BlockSpec((None, keep * KV_ROWS, HD_A), lambda n: (n, 0, 0)))
        out_shape.append(jax.ShapeDtypeStruct((n_seq, keep * KV_ROWS, HD_A), F32))
    return pl.pallas_call(
        _win_kernel,
        grid=(n_seq,),
        in_specs=in_specs,
        out_specs=out_specs,
        out_shape=out_shape,
        compiler_params=_params("parallel"),
        name="prompt_windows",
    )(*([qkv] * (2 * N_GROUPS)))


def _attn_sample_kernel(qkv_ref, c0_ref, c1_ref, c2_ref, o_ref, w0_ref, w1_ref, w2_ref, *, t_new):
    caches = (c0_ref, c1_ref, c2_ref)
    wins = (w0_ref, w1_ref, w2_ref)
    hw = H_G * HD_A
    accs = [[] for _ in range(N_GROUPS)]
    ms = [[] for _ in range(N_GROUPS)]
    ls = [[] for _ in range(N_GROUPS)]
    for g in range(N_GROUPS):
        c_ref, w_ref = caches[g], wins[g]
        n_buf = c_ref.shape[0] // KV_ROWS
        dil = DILATIONS[g]
        n_back = WINDOWS[g] // dil
        base = g * 3 * hw
        keep_rows = (n_buf - t_new) * KV_ROWS
        w_ref[pl.ds(0, keep_rows), :] = c_ref[pl.ds(t_new * KV_ROWS, keep_rows), :]
        for c in range(2):
            for h in range(H_G):
                col = base + (1 + c) * hw + h * HD_A
                w_ref[pl.ds(keep_rows + c * H_G + h, t_new, stride=KV_ROWS), :] = qkv_ref[:, col:col + HD_A]

        t_w = lax.broadcasted_iota(jnp.int32, (t_new, n_buf), 0)
        c_w = lax.broadcasted_iota(jnp.int32, (t_new, n_buf), 1)
        diff_w = n_buf + t_w - (c_w + t_new)
        valid_w = (diff_w >= 0) & ((diff_w & (dil - 1)) == 0) & (diff_w <= dil * n_back)
        t_o = lax.broadcasted_iota(jnp.int32, (t_new, LANE), 0)
        c_o = lax.broadcasted_iota(jnp.int32, (t_new, LANE), 1)
        diff_o = n_buf + t_o - c_o
        valid_o = (c_o < t_new) & ((diff_o & (dil - 1)) == 0) & (diff_o <= dil * n_back)

        for h in range(H_G):
            q = qkv_ref[:, base + h * HD_A: base + (h + 1) * HD_A].astype(BF16)
            k_w = w_ref[pl.ds(h, n_buf, stride=KV_ROWS), :].astype(BF16)
            v_w = w_ref[pl.ds(H_G + h, n_buf, stride=KV_ROWS), :].astype(BF16)
            k_o = c_ref[pl.ds(h, LANE, stride=KV_ROWS), :].astype(BF16)
            v_o = c_ref[pl.ds(H_G + h, LANE, stride=KV_ROWS), :].astype(BF16)
            s_w = jnp.where(valid_w, _dot_nt(q, k_w) * (HD_A ** -0.5), NEG)
            s_o = jnp.where(valid_o, _dot_nt(q, k_o) * (HD_A ** -0.5), NEG)
            m = jnp.maximum(jnp.max(s_w, axis=-1, keepdims=True), jnp.max(s_o, axis=-1, keepdims=True))
            p_w = jnp.exp(s_w - m)
            p_o = jnp.exp(s_o - m)
            l = jnp.sum(p_w, axis=-1, keepdims=True) + jnp.sum(p_o, axis=-1, keepdims=True)
            acc = _dot(p_w.astype(BF16), v_w) + _dot(p_o.astype(BF16), v_o)
            accs[g].append(acc)
            ms[g].append(m)
            ls[g].append(l)

    for h in range(H_G):
        m = jnp.maximum(jnp.maximum(ms[0][h], ms[1][h]), ms[2][h])
        num = jnp.zeros((t_new, HD_A), F32)
        den = jnp.zeros((t_new, 1), F32)
        for g in range(N_GROUPS):
            e = jnp.exp(ms[g][h] - m)
            num = num + e * accs[g][h]
            den = den + e * ls[g][h]
        o_ref[:, h * HD_A:(h + 1) * HD_A] = num / den


def _attn_sample(qkv, caches, layer, n_seq, t_new):
    in_specs = [pl.BlockSpec((t_new, qkv.shape[1]), lambda n: (n, 0))]
    out_specs = [pl.BlockSpec((t_new, H_G * HD_A), lambda n: (n, 0))]
    out_shape = [jax.ShapeDtypeStruct((n_seq * t_new, H_G * HD_A), F32)]
    for c in caches:
        rows = c.shape[2]
        in_specs.append(pl.BlockSpec((None, None, rows, HD_A), lambda n: (layer, n, 0, 0)))
        out_specs.append(pl.BlockSpec((None, rows, HD_A), lambda n: (n, 0, 0)))
        out_shape.append(jax.ShapeDtypeStruct((n_seq, rows, HD_A), F32))
    return pl.pallas_call(
        functools.partial(_attn_sample_kernel, t_new=t_new),
        grid=(n_seq,),
        in_specs=in_specs,
        out_specs=out_specs,
        out_shape=out_shape,
        compiler_params=_params("parallel"),
        name="attn_sample",
    )(qkv, *caches)


def _softmax_pv(q, kb, vb, scale):
    s = _dot_nt(q, kb) * scale
    m = jnp.max(s, axis=-1, keepdims=True)
    p = jnp.exp(s - m)
    l = jnp.sum(p, axis=-1, keepdims=True)
    return _dot(p.astype(BF16), vb) / l


def _mem_attn_prompt_kernel(q_ref, k_ref, v_ref, o_ref, kv_ref):
    rows, hd = q_ref.shape
    h = pl.program_id(1)
    k = k_ref[...]
    v = v_ref[...]
    for hh in range(H_M):
        @pl.when(h == hh)
        def _(hh=hh):
            kv_ref[:, 0, hh, :] = k
            kv_ref[:, 1, hh, :] = v
    kb = k.astype(BF16)
    vb = v.astype(BF16)
    chunk = min(rows, 256)

    def body(c, carry):
        r = pl.ds(pl.multiple_of(c * chunk, chunk), chunk)
        o_ref[r, :] = _softmax_pv(q_ref[r, :], kb, vb, hd ** -0.5).astype(o_ref.dtype)
        return carry

    lax.fori_loop(0, rows // chunk, body, 0, unroll=True)


def _mem_attn_prompt(q, kv, n_seq, rows):
    hd = q.shape[1] // H_M
    n_mem = kv.shape[0] // n_seq
    return pl.pallas_call(
        _mem_attn_prompt_kernel,
        grid=(n_seq, H_M),
        in_specs=[pl.BlockSpec((rows, hd), lambda n, h: (n, h)),
                  pl.BlockSpec((n_mem, hd), lambda n, h: (n, h)),
                  pl.BlockSpec((n_mem, hd), lambda n, h: (n, H_M + h))],
        out_specs=[pl.BlockSpec((rows, hd), lambda n, h: (n, h)),
                   pl.BlockSpec((None, n_mem, 2, H_M, hd), lambda n, h: (n, 0, 0, 0, 0))],
        out_shape=[jax.ShapeDtypeStruct(q.shape, BF16),
                   jax.ShapeDtypeStruct((n_seq, n_mem, 2, H_M, hd), F32)],
        compiler_params=_params("parallel", "arbitrary"),
        name="mem_attn_prompt",
    )(q, kv, kv)


def _mem_attn_sample_kernel(q_ref, kv_ref, o_ref):
    hd = kv_ref.shape[-1]
    for h in range(H_M):
        kb = kv_ref[:, 0, h, :].astype(BF16)
        vb = kv_ref[:, 1, h, :].astype(BF16)
        q = q_ref[:, h * hd:(h + 1) * hd].astype(BF16)
        o_ref[:, h * hd:(h + 1) * hd] = _softmax_pv(q, kb, vb, hd ** -0.5)


def _mem_attn_sample(q, cache, layer, n_seq, rows):
    n_mem, _, _, hd = cache.shape[2:]
    return pl.pallas_call(
        _mem_attn_sample_kernel,
        grid=(n_seq,),
        in_specs=[pl.BlockSpec((rows, q.shape[1]), lambda n: (n, 0)),
                  pl.BlockSpec((None, None, n_mem, 2, H_M, hd), lambda n: (layer, n, 0, 0, 0, 0))],
        out_specs=pl.BlockSpec((rows, q.shape[1]), lambda n: (n, 0)),
        out_shape=jax.ShapeDtypeStruct(q.shape, F32),
        compiler_params=_params("parallel"),
        name="mem_attn_sample",
    )(q, cache)


def kernel(x_prompt, x_sample, mem_prompt, cache_win0_kv, cache_win1_kv, cache_win2_kv, state_conv_b, state_ffn_conv, cache_mem_kv, g_mix, g_mem_q, g_mem_kv, g_ffn, g_final, w_in_a, w_out_a, w_in_b, conv_b, w_out_b, w_q_mem, w_kv_mem, w_o_mem, w_up, conv_ffn, w_down):
    n_p, seq, d = x_prompt.shape
    n_s, t_new, _ = x_sample.shape
    depth = g_mix.shape[0]
    n_mem = mem_prompt.shape[1]

    yp = x_prompt.reshape(n_p * seq, d)
    ys = x_sample.reshape(n_s * t_new, d)
    mem = mem_prompt.reshape(n_p * n_mem, d)
    caches = [c.reshape(c.shape[0], c.shape[1], c.shape[2] * KV_ROWS, HD_A)
              for c in (cache_win0_kv, cache_win1_kv, cache_win2_kv)]

    win_p = [[] for _ in range(N_GROUPS)]
    win_s = [[] for _ in range(N_GROUPS)]
    conv_p, conv_s, ffn_p, ffn_s, mem_p = [], [], [], [], []

    for i in range(depth):
        j = i // 2
        hp = _rmsnorm(yp, g_mix[i], BF16)
        hs = _rmsnorm(ys, g_mix[i], BF16)
        if i % 2 == 0:
            qkv_p, qkv_s = _proj(hp, hs, w_in_a, j, bm=512, bn=3 * H_G * HD_A)
            o_p = _attn_prompt(qkv_p, n_p, seq)
            o_s, *new_wins = _attn_sample(qkv_s, caches, j, n_s, t_new)
            yp, ys = _proj(o_p, o_s, w_out_a, j, res=(yp, ys))
            wins = _prompt_windows(qkv_p, n_p, seq)
            for g in range(N_GROUPS):
                win_p[g].append(wins[g].reshape(n_p, -1, 2, H_G, HD_A))
                win_s[g].append(new_wins[g].reshape(n_s, -1, 2, H_G, HD_A))
        else:
            o_p, o_s, st_p, st_s = _mix_b(hp, hs, w_in_b, conv_b, state_conv_b, j, n_p, seq, n_s, t_new)
            yp, ys = _proj(o_p, o_s, w_out_b, j, res=(yp, ys))
            conv_p.append(st_p)
            conv_s.append(st_s)

        kv_p = _matmul(_rmsnorm(mem, g_mem_kv[i], BF16), w_kv_mem, i, bm=n_p * n_mem, bn=512)
        q_p, q_s = _proj(_rmsnorm(yp, g_mem_q[i], BF16), _rmsnorm(ys, g_mem_q[i], BF16), w_q_mem, i,
                         out_dtype=BF16)
        a_p, kv_out = _mem_attn_prompt(q_p, kv_p, n_p, seq)
        mem_p.append(kv_out)
        a_s = _mem_attn_sample(q_s, cache_mem_kv, i, n_s, t_new)
        yp, ys = _proj(a_p, a_s, w_o_mem, i, res=(yp, ys))

        f_p, f_s, sf_p, sf_s = _ffn_up(_rmsnorm(yp, g_ffn[i], BF16), _rmsnorm(ys, g_ffn[i], BF16),
                                       w_up, conv_ffn, state_ffn_conv, i, n_p, seq, n_s, t_new)
        ffn_p.append(sf_p)
        ffn_s.append(sf_s)
        yp, ys = _proj(f_p, f_s, w_down, i, res=(yp, ys), bm=512)

    y_prompt = _rmsnorm(yp, g_final, F32).reshape(n_p, seq, d)
    y_sample = _rmsnorm(ys, g_final, F32).reshape(n_s, t_new, d)
    return (y_prompt, y_sample,
            jnp.stack(win_p[0]), jnp.stack(win_p[1]), jnp.stack(win_p[2]),
            jnp.stack(conv_p), jnp.stack(ffn_p), jnp.stack(mem_p),
            jnp.stack(win_s[0]), jnp.stack(win_s[1]), jnp.stack(win_s[2]),
            jnp.stack(conv_s), jnp.stack(ffn_s))
```
